```python
import math
import jax, jax.numpy as jnp
from jax import lax
import numpy as np

D_MODEL = 2048
BATCH = 4
SEQ = 2048
DEPTH = 4
DEC_BATCH = 128
DEC_SEQ = 1
PAST_LEN = 16384
PAGE_SIZE = 128

N_BRANCH = 4
BRANCH_WIDTH = D_MODEL // 4
RET_HEADS = 4
RET_DK = BRANCH_WIDTH // RET_HEADS
RET_DV = BRANCH_WIDTH // RET_HEADS
ROPE_BASE = 10000.0
GDN_HEADS = 4
GDN_DK = BRANCH_WIDTH // GDN_HEADS
GDN_DV = BRANCH_WIDTH // GDN_HEADS
GDN_CONV = 4
SCONV_W = 3
POOL_WINDOWS = (2, 4, 8, 16)
POOL_GROUPS = 4
POOL_GDIM = BRANCH_WIDTH // POOL_GROUPS
POOL_HIST = 15
CHUNK = 64
EPS = 1e-6
SPLIT_SIZES = (BRANCH_WIDTH, BRANCH_WIDTH, BRANCH_WIDTH, BRANCH_WIDTH,
               3 * BRANCH_WIDTH, BRANCH_WIDTH, GDN_HEADS, GDN_HEADS,
               BRANCH_WIDTH, BRANCH_WIDTH, BRANCH_WIDTH, BRANCH_WIDTH,
               BRANCH_WIDTH, BRANCH_WIDTH, N_BRANCH * D_MODEL)
IN_COLS = 14 * BRANCH_WIDTH + 2 * GDN_HEADS + N_BRANCH * D_MODEL

kernel_name = "hybrid_retention_gdn_conv_pool_step"


def _rms_norm(x, g):
    xf = x.astype(jnp.float32)
    y = xf * lax.rsqrt(jnp.mean(xf * xf, axis=-1, keepdims=True) + EPS)
    return (y * g.astype(jnp.float32)).astype(x.dtype)


def _head_rms(x):
    xf = x.astype(jnp.float32)
    return (xf * lax.rsqrt(jnp.mean(xf * xf, axis=-1, keepdims=True) + EPS)).astype(x.dtype)


def _l2norm(x):
    xf = x.astype(jnp.float32)
    return (xf * lax.rsqrt(jnp.sum(xf * xf, axis=-1, keepdims=True) + EPS)).astype(x.dtype)


def _rope(x, pos):
    half = x.shape[-1] // 2
    inv = ROPE_BASE ** (-jnp.arange(half, dtype=jnp.float32) / half)
    ang = pos.astype(jnp.float32)[:, None] * inv[None, :]
    cos = jnp.cos(ang)[None, :, None, :]
    sin = jnp.sin(ang)[None, :, None, :]
    xf = x.astype(jnp.float32)
    x1, x2 = xf[..., :half], xf[..., half:]
    return jnp.concatenate([x1 * cos - x2 * sin, x1 * sin + x2 * cos], axis=-1).astype(x.dtype)


def _causal_dwconv(u, hist, w):
    K = w.shape[0]
    T = u.shape[1]
    ext = jnp.concatenate([hist.astype(u.dtype), u], axis=1)
    out = ext[:, 0:T] * w[0]
    for k in range(1, K):
        out = out + ext[:, k:k + T] * w[k]
    return out, ext[:, ext.shape[1] - (K - 1):]


def _to_chunks(a, C, n):
    pad = n * C - a.shape[2]
    a = jnp.pad(a, [(0, 0), (0, 0), (0, pad)] + [(0, 0)] * (a.ndim - 3))
    return a.reshape(a.shape[:2] + (n, C) + a.shape[3:])


def _retention(q, k, v, g, s0):
    f32 = jnp.float32
    T = q.shape[2]
    C = min(CHUNK, T)
    n = -(-T // C)
    qc, kc, vc = (_to_chunks(a.astype(f32), C, n) for a in (q, k, v))
    b = jnp.cumsum(_to_chunks(g.astype(f32), C, n), axis=-1)
    causal = jnp.tril(jnp.ones((C, C), dtype=bool))
    dmask = jnp.exp(jnp.where(causal, b[..., :, None] - b[..., None, :], -jnp.inf))
    scores = jnp.einsum('bhnid,bhnjd->bhnij', qc, kc) * dmask
    o_intra = jnp.einsum('bhnij,bhnje->bhnie', scores, vc)
    b_last = b[..., -1]
    kv = jnp.einsum('bhnj,bhnjd,bhnje->bhnde', jnp.exp(b_last[..., None] - b), kc, vc)

    def step(s, inp):
        dec, kv_n = inp
        return s * dec[..., None, None] + kv_n, s

    s_fin, s_prev = lax.scan(step, s0.astype(f32),
                             (jnp.moveaxis(jnp.exp(b_last), 2, 0), jnp.moveaxis(kv, 2, 0)))
    s_prev = jnp.moveaxis(s_prev, 0, 2)
    o_inter = jnp.einsum('bhnid,bhnde->bhnie', qc * jnp.exp(b)[..., None], s_prev)
    o = (o_intra + o_inter).reshape(qc.shape[:2] + (n * C, vc.shape[-1]))[:, :, :T]
    return o, s_fin


def _gated_delta(q, k, v, g, beta, s0):
    f32 = jnp.float32
    T = q.shape[2]
    dv = v.shape[-1]
    C = min(CHUNK, T)
    n = -(-T // C)
    qc, kc, vc = (_to_chunks(a.astype(f32), C, n) for a in (q, k, v))
    bc = _to_chunks(beta.astype(f32), C, n)
    b = jnp.cumsum(_to_chunks(g.astype(f32), C, n), axis=-1)
    causal = jnp.tril(jnp.ones((C, C), dtype=bool))
    strict = jnp.tril(jnp.ones((C, C), dtype=bool), k=-1)
    decay = jnp.exp(jnp.where(causal, b[..., :, None] - b[..., None, :], -jnp.inf))
    kb = kc * bc[..., None]
    L = jnp.where(strict, jnp.einsum('bhnid,bhnjd->bhnij', kb, kc) * decay, 0.0)
    A = L + jnp.eye(C, dtype=f32)
    rhs = jnp.concatenate([vc * bc[..., None], kb * jnp.exp(b)[..., None]], axis=-1)
    sol = lax.linalg.triangular_solve(A, rhs, left_side=True, lower=True, unit_diagonal=True)
    u, w = sol[..., :dv], sol[..., dv:]
    attn = jnp.einsum('bhnid,bhnjd->bhnij', qc, kc) * decay
    q_dec = qc * jnp.exp(b)[..., None]
    b_last = b[..., -1]
    k_dec = kc * jnp.exp(b_last[..., None] - b)[..., None]

    def step(s, inp):
        u_n, w_n, attn_n, qd_n, kd_n, dl_n = inp
        v_new = u_n - jnp.einsum('bhcd,bhde->bhce', w_n, s)
        o = jnp.einsum('bhcd,bhde->bhce', qd_n, s) + jnp.einsum('bhij,bhje->bhie', attn_n, v_new)
        s = s * dl_n[..., None, None] + jnp.einsum('bhcd,bhce->bhde', kd_n, v_new)
        return s, o

    xs = tuple(jnp.moveaxis(a, 2, 0) for a in (u, w, attn, q_dec, k_dec, jnp.exp(b_last)))
    s_fin, o = lax.scan(step, s0.astype(f32), xs)
    o = jnp.moveaxis(o, 0, 2).reshape(qc.shape[:2] + (n * C, dv))[:, :, :T]
    return o, s_fin


def _layer(x, pos, s_ret, s_gdn, h_gdn, h_sc, h_pool,
           norm_g, w_in, gdn_conv_w, gdn_a_log, gdn_dt_bias, gdn_norm_g,
           sconv_w, pool_w, pool_scale, w_branch, w_out):
    Bn, T, _ = x.shape
    BW = BRANCH_WIDTH
    silu = jax.nn.silu
    h = _rms_norm(x, norm_g)
    z = h @ w_in
    idx, acc = [], 0
    for s in SPLIT_SIZES[:-1]:
        acc += s
        idx.append(acc)
    (ret_q, ret_k, ret_v, ret_gate, gdn_qkv, gdn_gate, gdn_a, gdn_b,
     sc_b, sc_c, sc_h, sc_gate, pool_u, pool_gate, merge_logits) = jnp.split(z, idx, axis=-1)

    q = _rope(ret_q.reshape(Bn, T, RET_HEADS, RET_DK), pos)
    k = _rope(ret_k.reshape(Bn, T, RET_HEADS, RET_DK), pos) * (RET_DK ** -0.5)
    v = ret_v.reshape(Bn, T, RET_HEADS, RET_DV)
    log_gamma = jnp.log(1.0 - 2.0 ** (-5.0 - jnp.arange(RET_HEADS, dtype=jnp.float32)))
    g_ret = jnp.broadcast_to(log_gamma[None, :, None], (Bn, RET_HEADS, T))
    o, s_ret_new = _retention(q.transpose(0, 2, 1, 3), k.transpose(0, 2, 1, 3),
                              v.transpose(0, 2, 1, 3), g_ret, s_ret)
    o = _head_rms(o.transpose(0, 2, 1, 3)).astype(x.dtype)
    y_ret = o.reshape(Bn, T, BW) * silu(ret_gate)

    qkv, h_gdn_new = _causal_dwconv(gdn_qkv, h_gdn, gdn_conv_w)
    qkv = silu(qkv)
    gq, gk, gv = jnp.split(qkv, 3, axis=-1)
    gq = _l2norm(gq.reshape(Bn, T, GDN_HEADS, GDN_DK)) * (GDN_DK ** -0.5)
    gk = _l2norm(gk.reshape(Bn, T, GDN_HEADS, GDN_DK))
    gv = gv.reshape(Bn, T, GDN_HEADS, GDN_DV)
    beta = jax.nn.sigmoid(gdn_b.astype(jnp.float32))
    g_gdn = -jnp.exp(gdn_a_log.astype(jnp.float32)) * jax.nn.softplus(
        gdn_a.astype(jnp.float32) + gdn_dt_bias.astype(jnp.float32))
    o, s_gdn_new = _gated_delta(gq.transpose(0, 2, 1, 3), gk.transpose(0, 2, 1, 3),
                                gv.transpose(0, 2, 1, 3), g_gdn.transpose(0, 2, 1),
                                beta.transpose(0, 2, 1), s_gdn)
    o = _rms_norm(o.transpose(0, 2, 1, 3).astype(x.dtype), gdn_norm_g)
    y_gdn = o.reshape(Bn, T, BW) * silu(gdn_gate)

    conv, h_sc_new = _causal_dwconv(sc_c * sc_h, h_sc, sconv_w)
    y_sc = sc_b * conv * silu(sc_gate)

    ext = jnp.concatenate([h_pool.astype(pool_u.dtype), pool_u], axis=1)
    cs = jnp.pad(jnp.cumsum(ext.astype(jnp.float32), axis=1), ((0, 0), (1, 0), (0, 0)))
    hi = cs[:, POOL_HIST + 1:]
    means = []
    for gi, wnd in enumerate(POOL_WINDOWS):
        c0, c1 = gi * POOL_GDIM, (gi + 1) * POOL_GDIM
        lo = cs[:, POOL_HIST + 1 - wnd:POOL_HIST + 1 - wnd + T, c0:c1]
        cnt = jnp.minimum(pos + 1, wnd).astype(jnp.float32)[None, :, None]
        means.append((hi[..., c0:c1] - lo) / cnt)
    pooled = jnp.concatenate(means, axis=-1).astype(pool_u.dtype) - pool_u
    mixed = jnp.einsum('btgc,gce->btge', pooled.reshape(Bn, T, POOL_GROUPS, POOL_GDIM), pool_w)
    y_pool = mixed.reshape(Bn, T, BW) * pool_scale * silu(pool_gate)
    h_pool_new = ext[:, ext.shape[1] - POOL_HIST:]

    ys = jnp.stack([y_ret, y_gdn, y_sc, y_pool], axis=2)
    proj = jnp.einsum('btnc,ncd->btnd', ys, w_branch)
    gates = jax.nn.sigmoid(merge_logits.reshape(Bn, T, N_BRANCH, D_MODEL))
    merged = jnp.sum(gates * proj, axis=2)
    out = x + merged @ w_out
    return out, (s_ret_new.astype(s_ret.dtype), s_gdn_new.astype(s_gdn.dtype),
                 h_gdn_new, h_sc_new, h_pool_new)


def setup_inputs(seed: int = 0) -> dict:
    key = jax.random.key(seed)
    ks = jax.random.split(key, 20)
    f32 = jnp.float32
    BW = BRANCH_WIDTH

    def nrm(k, shape, s):
        return jax.random.normal(k, shape, f32) * s

    dt = jnp.exp(jax.random.uniform(ks[11], (DEPTH, GDN_HEADS), f32, math.log(1e-3), math.log(1e-1)))
    return {
        "x_prompt": nrm(ks[0], (BATCH, SEQ, D_MODEL), 1.0),
        "x_sample": nrm(ks[1], (DEC_BATCH, DEC_SEQ, D_MODEL), 1.0),
        "state_ret": nrm(ks[2], (DEPTH, DEC_BATCH, RET_HEADS, RET_DK, RET_DV), 0.5),
        "state_gdn": nrm(ks[3], (DEPTH, DEC_BATCH, GDN_HEADS, GDN_DK, GDN_DV), 0.1),
        "state_gdn_conv": nrm(ks[4], (DEPTH, DEC_BATCH, GDN_CONV - 1, 3 * BW), 1.0),
        "state_sconv": nrm(ks[5], (DEPTH, DEC_BATCH, SCONV_W - 1, BW), 1.0),
        "state_pool": nrm(ks[6], (DEPTH, DEC_BATCH, POOL_HIST, BW), 1.0),
        "norm_g": 1.0 + nrm(ks[7], (DEPTH, D_MODEL), 0.02),
        "w_in": nrm(ks[8], (DEPTH, D_MODEL, IN_COLS), D_MODEL ** -0.5),
        "gdn_conv_w": nrm(ks[9], (DEPTH, GDN_CONV, 3 * BW), GDN_CONV ** -0.5),
        "gdn_a_log": jnp.log(jax.random.uniform(ks[10], (DEPTH, GDN_HEADS), f32, 1.0, 16.0)),
        "gdn_dt_bias": dt + jnp.log(-jnp.expm1(-dt)),
        "gdn_norm_g": 1.0 + nrm(ks[12], (DEPTH, GDN_DV), 0.02),
        "sconv_w": nrm(ks[13], (DEPTH, SCONV_W, BW), SCONV_W ** -0.5),
        "pool_w": nrm(ks[14], (DEPTH, POOL_GROUPS, POOL_GDIM, POOL_GDIM), POOL_GDIM ** -0.5),
        "pool_scale": 1.0 + nrm(ks[15], (DEPTH, BW), 0.1),
        "w_branch": nrm(ks[16], (DEPTH, N_BRANCH, BW, D_MODEL), BW ** -0.5),
        "w_out": nrm(ks[17], (DEPTH, D_MODEL, D_MODEL), D_MODEL ** -0.5),
        "final_norm_g": 1.0 + nrm(ks[18], (D_MODEL,), 0.02),
    }


def reference(x_prompt, x_sample, state_ret, state_gdn, state_gdn_conv, state_sconv, state_pool,
              norm_g, w_in, gdn_conv_w, gdn_a_log, gdn_dt_bias, gdn_norm_g,
              sconv_w, pool_w, pool_scale, w_branch, w_out, final_norm_g):
    BW = BRANCH_WIDTH
    Bp, Tp, _ = x_prompt.shape
    Ts = x_sample.shape[1]
    pos_p = jnp.arange(Tp, dtype=jnp.int32)
    pos_s = PAST_LEN + jnp.arange(Ts, dtype=jnp.int32)
    z_ret = jnp.zeros((Bp, RET_HEADS, RET_DK, RET_DV), state_ret.dtype)
    z_gdn = jnp.zeros((Bp, GDN_HEADS, GDN_DK, GDN_DV), state_gdn.dtype)
    z_gconv = jnp.zeros((Bp, GDN_CONV - 1, 3 * BW), x_prompt.dtype)
    z_sconv = jnp.zeros((Bp, SCONV_W - 1, BW), x_prompt.dtype)
    z_pool = jnp.zeros((Bp, POOL_HIST, BW), x_prompt.dtype)

    hp, hs = x_prompt, x_sample
    new_p = [[], [], [], [], []]
    new_s = [[], [], [], [], []]
    for l in range(DEPTH):
        wts = (norm_g[l], w_in[l], gdn_conv_w[l], gdn_a_log[l], gdn_dt_bias[l], gdn_norm_g[l],
               sconv_w[l], pool_w[l], pool_scale[l], w_branch[l], w_out[l])
        hp, st_p = _layer(hp, pos_p, z_ret, z_gdn, z_gconv, z_sconv, z_pool, *wts)
        hs, st_s = _layer(hs, pos_s, state_ret[l], state_gdn[l], state_gdn_conv[l],
                          state_sconv[l], state_pool[l], *wts)
        for i in range(5):
            new_p[i].append(st_p[i])
            new_s[i].append(st_s[i])
    p_ret, p_gdn, p_gconv, p_sconv, p_pool = (jnp.stack(a, axis=0) for a in new_p)
    s_ret, s_gdn, s_gconv, s_sconv, s_pool = (jnp.stack(a, axis=0) for a in new_s)
    y_prompt = _rms_norm(hp, final_norm_g)
    y_sample = _rms_norm(hs, final_norm_g)
    return (y_prompt, y_sample, p_ret, p_gdn, p_gconv, p_sconv, p_pool,
            s_ret, s_gdn, s_gconv, s_sconv, s_pool)
```

```python
import functools
import math

import jax
import jax.numpy as jnp
from jax import lax
from jax.experimental import pallas as pl
from jax.experimental.pallas import tpu as pltpu

F32 = jnp.float32
BF16 = jnp.bfloat16

D_MODEL = 2048
BW = D_MODEL // 4
HEADS = 4
HD = BW // HEADS
ROPE_BASE = 10000.0
GDN_CONV = 4
SCONV_W = 3
POOL_WINDOWS = (2, 4, 8, 16)
POOL_HIST = 15
GCHUNK = 64
EPS = 1e-6
PAST_LEN = 16384
N_MIX = 14 * BW
LANES = 128
VMEM_LIMIT = 48 * 1024 * 1024

C_RET_Q, C_RET_K, C_RET_V, C_RET_G = 0, BW, 2 * BW, 3 * BW
C_GDN_QKV, C_GDN_G = 4 * BW, 7 * BW
C_SC_B, C_SC_C, C_SC_H, C_SC_G = 8 * BW, 9 * BW, 10 * BW, 11 * BW
C_POOL_U, C_POOL_G = 12 * BW, 13 * BW


def _dot(a, b):
    return jnp.dot(a, b, preferred_element_type=F32)


def _dot_nt(a, b):
    return lax.dot_general(a, b, (((1,), (1,)), ((), ())), preferred_element_type=F32)


def _dot_tn(a, b):
    return lax.dot_general(a, b, (((0,), (0,)), ((), ())), preferred_element_type=F32)


def _dot_hi(a, b):
    return jnp.dot(a, b, preferred_element_type=F32, precision=lax.Precision.HIGHEST)


def _silu(x):
    return x * jax.nn.sigmoid(x)


def _softplus(x):
    return jnp.maximum(x, 0.0) + jnp.log1p(jnp.exp(-jnp.abs(x)))


def _rms_rows(x):
    return x * lax.rsqrt(jnp.mean(x * x, axis=-1, keepdims=True) + EPS)


def _rope(x, cos, sin_signed):
    return x * cos + pltpu.roll(x, HD // 2, axis=1) * sin_signed


def _log_gamma(h):
    return math.log(1.0 - 2.0 ** (-5.0 - h))


def _in_proj_kernel(x_ref, g_ref, w_ref, wab_ref, z_ref, zab_ref, h_ref):
    @pl.when(pl.program_id(1) == 0)
    def _():
        hb = (_rms_rows(x_ref[...]) * g_ref[...]).astype(BF16)
        h_ref[...] = hb
        zab_ref[...] = _dot(hb, wab_ref[...])

    z_ref[...] = _dot(h_ref[...], w_ref[...])


def _in_proj(x, norm_g, w_mix, w_ab, layer, tm, tn=512):
    m = x.shape[0]
    return pl.pallas_call(
        _in_proj_kernel,
        grid=(m // tm, N_MIX // tn),
        in_specs=[
            pl.BlockSpec((tm, D_MODEL), lambda i, j: (i, 0)),
            pl.BlockSpec((None, 1, D_MODEL), lambda i, j: (layer, 0, 0)),
            pl.BlockSpec((None, D_MODEL, tn), lambda i, j: (layer, 0, j)),
            pl.BlockSpec((None, D_MODEL, LANES), lambda i, j: (layer, 0, 0)),
        ],
        out_specs=[
            pl.BlockSpec((tm, tn), lambda i, j: (i, j)),
            pl.BlockSpec((tm, LANES), lambda i, j: (i, 0)),
        ],
        out_shape=[
            jax.ShapeDtypeStruct((m, N_MIX), F32),
            jax.ShapeDtypeStruct((m, LANES), F32),
        ],
        scratch_shapes=[pltpu.VMEM((tm, D_MODEL), BF16)],
        compiler_params=pltpu.CompilerParams(
            dimension_semantics=("arbitrary", "arbitrary"), vmem_limit_bytes=VMEM_LIMIT),
        name="in_proj",
    )(x, norm_g, w_mix, w_ab)


def _neumann_inverse(l_mat, eye):
    p = eye - l_mat
    m = _dot_hi(l_mat, l_mat)
    steps = int(math.log2(GCHUNK)) - 1
    for s in range(steps):
        p = p + _dot_hi(p, m)
        if s + 1 < steps:
            m = _dot_hi(m, m)
    return p


def _mix_prompt_kernel(z_ref, zab_ref, cos_ref, sin_ref, convw_ref, alog_ref, dtb_ref, gng_ref,
                       scw_ref, poolw_ref, pscale_ref,
                       ys_ref, pret_ref, pgdn_ref, pgconv_ref, psconv_ref, ppool_ref,
                       sret, sgdn, extq, extsc, extpool, *, tt):
    t = pl.program_id(1)
    n_t = pl.num_programs(1)

    @pl.when(t == 0)
    def _():
        sret[...] = jnp.zeros_like(sret)
        sgdn[...] = jnp.zeros_like(sgdn)
        extq[0:8, :] = jnp.zeros((8, 3 * BW), F32)
        extsc[0:8, :] = jnp.zeros((8, BW), F32)
        extpool[0:16, :] = jnp.zeros((16, BW), F32)

    cos = cos_ref[...]
    sin = sin_ref[...]

    ii = lax.broadcasted_iota(jnp.int32, (tt, tt), 0)
    jj = lax.broadcasted_iota(jnp.int32, (tt, tt), 1)
    causal = ii >= jj
    dif = jnp.where(causal, ii - jj, 0).astype(F32)
    ipos = lax.broadcasted_iota(jnp.int32, (tt, 1), 0).astype(F32)
    for h in range(HEADS):
        lg = _log_gamma(h)
        hs = slice(h * HD, (h + 1) * HD)
        dmask = jnp.where(causal, jnp.exp(dif * lg), 0.0)
        q_scale = jnp.exp((ipos + 1.0) * lg)
        k_scale = jnp.exp((tt - 1.0 - ipos) * lg)
        q = _rope(z_ref[:, C_RET_Q + h * HD:C_RET_Q + (h + 1) * HD], cos, sin)
        k = _rope(z_ref[:, C_RET_K + h * HD:C_RET_K + (h + 1) * HD], cos, sin) * (HD ** -0.5)
        vb = z_ref[:, C_RET_V + h * HD:C_RET_V + (h + 1) * HD].astype(BF16)
        scores = _dot_nt(q.astype(BF16), k.astype(BF16)) * dmask
        s0 = sret[h]
        o = _dot(scores.astype(BF16), vb) + _dot((q * q_scale).astype(BF16), s0.astype(BF16))
        kv = _dot_tn((k * k_scale).astype(BF16), vb)
        sret[h] = s0 * math.exp(tt * lg) + kv
        ys_ref[:, hs] = _rms_rows(o) * _silu(z_ref[:, C_RET_G + h * HD:C_RET_G + (h + 1) * HD])

    extq[8:8 + tt, :] = z_ref[:, C_GDN_QKV:C_GDN_QKV + 3 * BW]
    conv = extq[5:5 + tt, :] * convw_ref[0:1, :]
    for kk in range(1, GDN_CONV):
        conv = conv + extq[5 + kk:5 + kk + tt, :] * convw_ref[kk:kk + 1, :]
    qkv = _silu(conv)
    extq[0:8, :] = extq[tt:tt + 8, :]

    zab = zab_ref[...]
    g_all = -jnp.exp(alog_ref[...]) * _softplus(zab + dtb_ref[...])
    beta_all = jax.nn.sigmoid(zab)
    rin = lax.broadcasted_iota(jnp.int32, (tt, LANES), 0) % GCHUNK
    b_all = g_all
    sh = 1
    while sh < GCHUNK:
        b_all = b_all + jnp.where(rin >= sh, pltpu.roll(b_all, sh, axis=0), 0.0)
        sh *= 2
    b_all_t = b_all.T

    ci = lax.broadcasted_iota(jnp.int32, (GCHUNK, GCHUNK), 0)
    cj = lax.broadcasted_iota(jnp.int32, (GCHUNK, GCHUNK), 1)
    c_causal = ci >= cj
    c_strict = ci > cj
    eye = jnp.where(ci == cj, 1.0, 0.0).astype(F32)

    o_heads = []
    for h in range(HEADS):
        hq = slice(h * HD, (h + 1) * HD)
        qh = qkv[:, hq]
        kh = qkv[:, BW + h * HD:BW + (h + 1) * HD]
        vh = qkv[:, 2 * BW + h * HD:2 * BW + (h + 1) * HD]
        qh = qh * lax.rsqrt(jnp.sum(qh * qh, axis=-1, keepdims=True) + EPS) * (HD ** -0.5)
        kh = kh * lax.rsqrt(jnp.sum(kh * kh, axis=-1, keepdims=True) + EPS)
        s = sgdn[h]
        o_chunks = []
        for c in range(tt // GCHUNK):
            rs = slice(c * GCHUNK, (c + 1) * GCHUNK)
            q, k, v = qh[rs], kh[rs], vh[rs]
            bcol = b_all[rs, h:h + 1]
            brow = b_all_t[h:h + 1, rs]
            beta = beta_all[rs, HEADS + h:HEADS + h + 1]
            blast = b_all[c * GCHUNK + GCHUNK - 1:(c + 1) * GCHUNK, h:h + 1]
            eb = jnp.exp(bcol)
            decay = jnp.where(c_causal, jnp.exp(jnp.where(c_causal, bcol - brow, 0.0)), 0.0)
            kb = k * beta
            k16 = k.astype(BF16)
            l_mat = jnp.where(c_strict, _dot_nt(kb.astype(BF16), k16) * decay, 0.0)
            attn = _dot_nt(q.astype(BF16), k16) * decay
            rhs = jnp.concatenate([v * beta, kb * eb], axis=1)
            sol = _dot_hi(_neumann_inverse(l_mat, eye), rhs)
            u, w = sol[:, :HD], sol[:, HD:]
            s16 = s.astype(BF16)
            v_new = u - _dot(w.astype(BF16), s16)
            o_chunks.append(_dot((q * eb).astype(BF16), s16)
                            + _dot(attn.astype(BF16), v_new.astype(BF16)))
            k_dec = k * jnp.exp(blast - bcol)
            s = s * jnp.exp(blast) + _dot_tn(k_dec.astype(BF16), v_new.astype(BF16))
        sgdn[h] = s
        o = jnp.concatenate(o_chunks, axis=0)
        ys_ref[:, BW + h * HD:BW + (h + 1) * HD] = (
            _rms_rows(o) * gng_ref[...] * _silu(z_ref[:, C_GDN_G + h * HD:C_GDN_G + (h + 1) * HD]))

    extsc[8:8 + tt, :] = z_ref[:, C_SC_C:C_SC_C + BW] * z_ref[:, C_SC_H:C_SC_H + BW]
    conv = extsc[6:6 + tt, :] * scw_ref[0:1, :]
    for kk in range(1, SCONV_W):
        conv = conv + extsc[6 + kk:6 + kk + tt, :] * scw_ref[kk:kk + 1, :]
    ys_ref[:, 2 * BW:3 * BW] = (z_ref[:, C_SC_B:C_SC_B + BW] * conv
                                * _silu(z_ref[:, C_SC_G:C_SC_G + BW]))
    extsc[0:8, :] = extsc[tt:tt + 8, :]

    extpool[16:16 + tt, :] = z_ref[:, C_POOL_U:C_POOL_U + BW]
    pos1 = (t * tt + lax.broadcasted_iota(jnp.int32, (tt, 1), 0) + 1).astype(F32)
    for gi, wnd in enumerate(POOL_WINDOWS):
        gs = slice(gi * HD, (gi + 1) * HD)
        cur = extpool[16:16 + tt, gs]
        acc = cur
        for d in range(1, wnd):
            acc = acc + extpool[16 - d:16 - d + tt, gs]
        pooled = acc / jnp.minimum(pos1, float(wnd)) - cur
        mixed = _dot(pooled.astype(BF16), poolw_ref[gi])
        ys_ref[:, 3 * BW + gi * HD:3 * BW + (gi + 1) * HD] = (
            mixed * pscale_ref[:, gs] * _silu(z_ref[:, C_POOL_G + gi * HD:C_POOL_G + (gi + 1) * HD]))
    extpool[0:16, :] = extpool[tt:tt + 16, :]

    @pl.when(t == n_t - 1)
    def _():
        pret_ref[...] = sret[...]
        pgdn_ref[...] = sgdn[...]
        pgconv_ref[...] = extq[8 - (GDN_CONV - 1):8, :]
        psconv_ref[...] = extsc[8 - (SCONV_W - 1):8, :]
        ppool_ref[...] = extpool[16 - POOL_HIST:16, :]


def _mix_prompt(z, zab, cos_t, sin_t, gdn_conv_w, alog_v, dtb_v, gdn_norm_g, sconv_w, pool_w16,
                pool_scale, layer, nb, nt, tt):
    full = lambda *shape: pl.BlockSpec((None,) + shape, lambda b, t: (layer,) + (0,) * len(shape))
    state = lambda *shape: pl.BlockSpec((None,) + shape, lambda b, t: (b,) + (0,) * len(shape))
    return pl.pallas_call(
        functools.partial(_mix_prompt_kernel, tt=tt),
        grid=(nb, nt),
        in_specs=[
            pl.BlockSpec((tt, N_MIX), lambda b, t: (b * nt + t, 0)),
            pl.BlockSpec((tt, LANES), lambda b, t: (b * nt + t, 0)),
            pl.BlockSpec((tt, HD), lambda b, t: (t, 0)),
            pl.BlockSpec((tt, HD), lambda b, t: (t, 0)),
            full(GDN_CONV, 3 * BW), full(1, LANES), full(1, LANES), full(1, HD),
            full(SCONV_W, BW), full(HEADS, HD, HD), full(1, BW),
        ],
        out_specs=[
            pl.BlockSpec((tt, D_MODEL), lambda b, t: (b * nt + t, 0)),
            state(HEADS, HD, HD), state(HEADS, HD, HD),
            state(GDN_CONV - 1, 3 * BW), state(SCONV_W - 1, BW), state(POOL_HIST, BW),
        ],
        out_shape=[
            jax.ShapeDtypeStruct((nb * nt * tt, D_MODEL), F32),
            jax.ShapeDtypeStruct((nb, HEADS, HD, HD), F32),
            jax.ShapeDtypeStruct((nb, HEADS, HD, HD), F32),
            jax.ShapeDtypeStruct((nb, GDN_CONV - 1, 3 * BW), F32),
            jax.ShapeDtypeStruct((nb, SCONV_W - 1, BW), F32),
            jax.ShapeDtypeStruct((nb, POOL_HIST, BW), F32),
        ],
        scratch_shapes=[
            pltpu.VMEM((HEADS, HD, HD), F32),
            pltpu.VMEM((HEADS, HD, HD), F32),
            pltpu.VMEM((tt + 8, 3 * BW), F32),
            pltpu.VMEM((tt + 8, BW), F32),
            pltpu.VMEM((tt + 16, BW), F32),
        ],
        compiler_params=pltpu.CompilerParams(
            dimension_semantics=("arbitrary", "arbitrary"), vmem_limit_bytes=VMEM_LIMIT),
        name="mix_prompt",
    )(z, zab, cos_t, sin_t, gdn_conv_w, alog_v, dtb_v, gdn_norm_g, sconv_w, pool_w16, pool_scale)


SB = 8


def _mix_sample_kernel(z_ref, zab_ref, cos_ref, sin_ref, convw_ref, alog_ref, dtb_ref, gng_ref,
                       scw_ref, poolw_ref, pscale_ref,
                       sret_in, sgdn_in, gconv_in, sconv_in, pool_in,
                       ys_ref, sret_out, sgdn_out, gconv_out, sconv_out, pool_out, o_scr):
    cos = cos_ref[...]
    sin = sin_ref[...]
    zero_rows = jnp.zeros((HD - SB, HD), F32)

    def columns(x):
        return jnp.concatenate([x, zero_rows], axis=0).T

    for h in range(HEADS):
        gamma = math.exp(_log_gamma(h))
        q = _rope(z_ref[:, C_RET_Q + h * HD:C_RET_Q + (h + 1) * HD], cos, sin)
        k = _rope(z_ref[:, C_RET_K + h * HD:C_RET_K + (h + 1) * HD], cos, sin) * (HD ** -0.5)
        v = z_ref[:, C_RET_V + h * HD:C_RET_V + (h + 1) * HD]
        q_t, k_t = columns(q), columns(k)
        for j in range(SB):
            s_new = sret_in[j, h] * gamma + k_t[:, j:j + 1] * v[j:j + 1, :]
            sret_out[j, h] = s_new
            o_scr[j:j + 1, h * HD:(h + 1) * HD] = jnp.sum(q_t[:, j:j + 1] * s_new, axis=0,
                                                          keepdims=True)

    u_qkv = z_ref[:, C_GDN_QKV:C_GDN_QKV + 3 * BW]
    conv = u_qkv * convw_ref[GDN_CONV - 1:GDN_CONV, :]
    for kk in range(GDN_CONV - 1):
        conv = conv + gconv_in[:, kk, :] * convw_ref[kk:kk + 1, :]
    qkv = _silu(conv)
    for kk in range(GDN_CONV - 2):
        gconv_out[:, kk, :] = gconv_in[:, kk + 1, :]
    gconv_out[:, GDN_CONV - 2, :] = u_qkv

    zab = zab_ref[...]
    decay_all = jnp.exp(-jnp.exp(alog_ref[...]) * _softplus(zab + dtb_ref[...]))
    beta_all = jax.nn.sigmoid(zab)
    for h in range(HEADS):
        qh = qkv[:, h * HD:(h + 1) * HD]
        kh = qkv[:, BW + h * HD:BW + (h + 1) * HD]
        vh = qkv[:, 2 * BW + h * HD:2 * BW + (h + 1) * HD]
        qh = qh * lax.rsqrt(jnp.sum(qh * qh, axis=-1, keepdims=True) + EPS) * (HD ** -0.5)
        kh = kh * lax.rsqrt(jnp.sum(kh * kh, axis=-1, keepdims=True) + EPS)
        q_t, k_t = columns(qh), columns(kh)
        for j in range(SB):
            a = decay_all[j:j + 1, h:h + 1]
            beta = beta_all[j:j + 1, HEADS + h:HEADS + h + 1]
            s_dec = sgdn_in[j, h] * a
            k_col = k_t[:, j:j + 1]
            v_new = (vh[j:j + 1, :] - jnp.sum(k_col * s_dec, axis=0, keepdims=True)) * beta
            s_new = s_dec + k_col * v_new
            sgdn_out[j, h] = s_new
            o_scr[j:j + 1, BW + h * HD:BW + (h + 1) * HD] = jnp.sum(q_t[:, j:j + 1] * s_new,
                                                                    axis=0, keepdims=True)

    for h in range(HEADS):
        hs = slice(h * HD, (h + 1) * HD)
        ys_ref[:, hs] = (_rms_rows(o_scr[:, hs])
                         * _silu(z_ref[:, C_RET_G + h * HD:C_RET_G + (h + 1) * HD]))
        ys_ref[:, BW + h * HD:BW + (h + 1) * HD] = (
            _rms_rows(o_scr[:, BW + h * HD:BW + (h + 1) * HD]) * gng_ref[...]
            * _silu(z_ref[:, C_GDN_G + h * HD:C_GDN_G + (h + 1) * HD]))

    prod = z_ref[:, C_SC_C:C_SC_C + BW] * z_ref[:, C_SC_H:C_SC_H + BW]
    conv = prod * scw_ref[SCONV_W - 1:SCONV_W, :]
    for kk in range(SCONV_W - 1):
        conv = conv + sconv_in[:, kk, :] * scw_ref[kk:kk + 1, :]
    ys_ref[:, 2 * BW:3 * BW] = (z_ref[:, C_SC_B:C_SC_B + BW] * conv
                                * _silu(z_ref[:, C_SC_G:C_SC_G + BW]))
    for kk in range(SCONV_W - 2):
        sconv_out[:, kk, :] = sconv_in[:, kk + 1, :]
    sconv_out[:, SCONV_W - 2, :] = prod

    pool_u = z_ref[:, C_POOL_U:C_POOL_U + BW]
    for gi, wnd in enumerate(POOL_WINDOWS):
        gs = slice(gi * HD, (gi + 1) * HD)
        cur = pool_u[:, gs]
        acc = cur
        for d in range(1, wnd):
            acc = acc + pool_in[:, POOL_HIST - d, gs]
        pooled = acc / float(min(PAST_LEN + 1, wnd)) - cur
        mixed = _dot(pooled.astype(BF16), poolw_ref[gi])
        ys_ref[:, 3 * BW + gi * HD:3 * BW + (gi + 1) * HD] = (
            mixed * pscale_ref[:, gs] * _silu(z_ref[:, C_POOL_G + gi * HD:C_POOL_G + (gi + 1) * HD]))
    for r in range(POOL_HIST - 1):
        pool_out[:, r, :] = pool_in[:, r + 1, :]
    pool_out[:, POOL_HIST - 1, :] = pool_u


def _mix_sample(z, zab, cos_t, sin_t, gdn_conv_w, alog_v, dtb_v, gdn_norm_g, sconv_w, pool_w16,
                pool_scale, state_ret, state_gdn, state_gdn_conv, state_sconv, state_pool, layer):
    ns = z.shape[0]
    full = lambda *shape: pl.BlockSpec((None,) + shape, lambda i: (layer,) + (0,) * len(shape))
    st_in = lambda *shape: pl.BlockSpec((None, SB) + shape, lambda i: (layer, i) + (0,) * len(shape))
    st_out = lambda *shape: pl.BlockSpec((SB,) + shape, lambda i: (i,) + (0,) * len(shape))
    return pl.pallas_call(
        _mix_sample_kernel,
        grid=(ns // SB,),
        in_specs=[
            pl.BlockSpec((SB, N_MIX), lambda i: (i, 0)),
            pl.BlockSpec((SB, LANES), lambda i: (i, 0)),
            pl.BlockSpec((1, HD), lambda i: (0, 0)),
            pl.BlockSpec((1, HD), lambda i: (0, 0)),
            full(GDN_CONV, 3 * BW), full(1, LANES), full(1, LANES), full(1, HD),
            full(SCONV_W, BW), full(HEADS, HD, HD), full(1, BW),
            st_in(HEADS, HD, HD), st_in(HEADS, HD, HD),
            st_in(GDN_CONV - 1, 3 * BW), st_in(SCONV_W - 1, BW), st_in(POOL_HIST, BW),
        ],
        out_specs=[
            pl.BlockSpec((SB, D_MODEL), lambda i: (i, 0)),
            st_out(HEADS, HD, HD), st_out(HEADS, HD, HD),
            st_out(GDN_CONV - 1, 3 * BW), st_out(SCONV_W - 1, BW), st_out(POOL_HIST, BW),
        ],
        out_shape=[
            jax.ShapeDtypeStruct((ns, D_MODEL), F32),
            jax.ShapeDtypeStruct((ns, HEADS, HD, HD), F32),
            jax.ShapeDtypeStruct((ns, HEADS, HD, HD), F32),
            jax.ShapeDtypeStruct((ns, GDN_CONV - 1, 3 * BW), F32),
            jax.ShapeDtypeStruct((ns, SCONV_W - 1, BW), F32),
            jax.ShapeDtypeStruct((ns, POOL_HIST, BW), F32),
        ],
        scratch_shapes=[pltpu.VMEM((SB, 2 * BW), F32)],
        compiler_params=pltpu.CompilerParams(
            dimension_semantics=("arbitrary",), vmem_limit_bytes=VMEM_LIMIT),
        name="mix_sample",
    )(z, zab, cos_t, sin_t, gdn_conv_w, alog_v, dtb_v, gdn_norm_g, sconv_w, pool_w16, pool_scale,
      state_ret, state_gdn, state_gdn_conv, state_sconv, state_pool)


TC = 512
NCT = D_MODEL // TC


def _out_kernel(x_ref, g_ref, ys_ref, wg0_ref, wg1_ref, wg2_ref, wg3_ref, wb_ref, wo_ref, xc_ref,
                o_ref, h_ref, merged_ref):
    j = pl.program_id(1)

    @pl.when(j == 0)
    def _():
        h_ref[...] = (_rms_rows(x_ref[...]) * g_ref[...]).astype(BF16)

    @pl.when(j < NCT)
    def _():
        hb = h_ref[...]
        acc = None
        for n, wg_ref in enumerate((wg0_ref, wg1_ref, wg2_ref, wg3_ref)):
            gate = jax.nn.sigmoid(_dot(hb, wg_ref[...]))
            proj = _dot(ys_ref[:, n * BW:(n + 1) * BW].astype(BF16), wb_ref[n])
            acc = gate * proj if acc is None else acc + gate * proj
        merged_ref[j] = acc.astype(BF16)

    @pl.when(j >= NCT)
    def _():
        acc = xc_ref[...]
        for c in range(NCT):
            acc = acc + _dot(merged_ref[c], wo_ref[c * TC:(c + 1) * TC, :])
        o_ref[...] = acc


def _out_proj(x, norm_g, ys, w_g, w_b, w_o, layer, tm):
    m = x.shape[0]
    first = lambda j: jnp.minimum(j, NCT - 1)
    second = lambda j: jnp.maximum(j - NCT, 0)
    gate_spec = lambda n: pl.BlockSpec((None, D_MODEL, TC),
                                       lambda i, j: (layer, 0, n * NCT + first(j)))
    return pl.pallas_call(
        _out_kernel,
        grid=(m // tm, 2 * NCT),
        in_specs=[
            pl.BlockSpec((tm, D_MODEL), lambda i, j: (i, 0)),
            pl.BlockSpec((None, 1, D_MODEL), lambda i, j: (layer, 0, 0)),
            pl.BlockSpec((tm, D_MODEL), lambda i, j: (i, 0)),
            gate_spec(0), gate_spec(1), gate_spec(2), gate_spec(3),
            pl.BlockSpec((None, 4, BW, TC), lambda i, j: (layer, 0, 0, first(j))),
            pl.BlockSpec((None, D_MODEL, TC), lambda i, j: (layer, 0, second(j))),
            pl.BlockSpec((tm, TC), lambda i, j: (i, second(j))),
        ],
        out_specs=pl.BlockSpec((tm, TC), lambda i, j: (i, second(j))),
        out_shape=jax.ShapeDtypeStruct((m, D_MODEL), F32),
        scratch_shapes=[pltpu.VMEM((tm, D_MODEL), BF16), pltpu.VMEM((NCT, tm, TC), BF16)],
        compiler_params=pltpu.CompilerParams(
            dimension_semantics=("arbitrary", "arbitrary"), vmem_limit_bytes=VMEM_LIMIT),
        name="out_proj",
    )(x, norm_g, ys, w_g, w_g, w_g, w_g, w_b, w_o, x)


def _final_norm_kernel(x_ref, g_ref, o_ref):
    o_ref[...] = _rms_rows(x_ref[...]) * g_ref[...]


def _final_norm(x, g, tm):
    m = x.shape[0]
    return pl.pallas_call(
        _final_norm_kernel,
        grid=(m // tm,),
        in_specs=[pl.BlockSpec((tm, D_MODEL), lambda i: (i, 0)),
                  pl.BlockSpec((1, D_MODEL), lambda i: (0, 0))],
        out_specs=pl.BlockSpec((tm, D_MODEL), lambda i: (i, 0)),
        out_shape=jax.ShapeDtypeStruct((m, D_MODEL), F32),
        compiler_params=pltpu.CompilerParams(dimension_semantics=("arbitrary",)),
        name="final_norm",
    )(x, g)


def _rope_tables(pos):
    half = HD // 2
    inv = ROPE_BASE ** (-jnp.arange(half, dtype=F32) / half)
    ang = pos.astype(F32)[:, None] * inv[None, :]
    cos, sin = jnp.cos(ang), jnp.sin(ang)
    return jnp.concatenate([cos, cos], axis=1), jnp.concatenate([-sin, sin], axis=1)


def kernel(x_prompt, x_sample, state_ret, state_gdn, state_gdn_conv, state_sconv, state_pool,
           norm_g, w_in, gdn_conv_w, gdn_a_log, gdn_dt_bias, gdn_norm_g, sconv_w, pool_w, pool_scale,
           w_branch, w_out, final_norm_g):
    bp, tp, _ = x_prompt.shape
    ns, ts, _ = x_sample.shape
    depth = w_in.shape[0]
    tt = min(256, tp)
    nt = tp // tt

    ab0 = 8 * BW
    w_mix = jnp.concatenate([w_in[:, :, :ab0], w_in[:, :, ab0 + 2 * HEADS:N_MIX + 2 * HEADS]],
                            axis=2).astype(BF16)
    w_ab = jnp.pad(w_in[:, :, ab0:ab0 + 2 * HEADS],
                   ((0, 0), (0, 0), (0, LANES - 2 * HEADS))).astype(BF16)
    w_g = w_in[:, :, N_MIX + 2 * HEADS:].astype(BF16)
    w_b = w_branch.astype(BF16)
    w_o = w_out.astype(BF16)
    pool_w16 = pool_w.astype(BF16)
    pad_h = lambda a: jnp.pad(a, ((0, 0), (0, LANES - HEADS)))[:, None, :]
    alog_v, dtb_v = pad_h(gdn_a_log), pad_h(gdn_dt_bias)
    norm_g3 = norm_g[:, None, :]
    gng3 = gdn_norm_g[:, None, :]
    pscale3 = pool_scale[:, None, :]

    cos_p, sin_p = _rope_tables(jnp.arange(tp, dtype=jnp.int32))
    cos_s, sin_s = _rope_tables(PAST_LEN + jnp.arange(ts, dtype=jnp.int32))

    hp = x_prompt.reshape(bp * tp, D_MODEL)
    hs = x_sample.reshape(ns * ts, D_MODEL)
    new_p = [[] for _ in range(5)]
    new_s = [[] for _ in range(5)]
    for l in range(depth):
        z, zab = _in_proj(hp, norm_g3, w_mix, w_ab, l, tm=512)
        ys, *st_p = _mix_prompt(z, zab, cos_p, sin_p, gdn_conv_w, alog_v, dtb_v, gng3, sconv_w,
                                pool_w16, pscale3, l, bp, nt, tt)
        hp = _out_proj(hp, norm_g3, ys, w_g, w_b, w_o, l, tm=256)

        z, zab = _in_proj(hs, norm_g3, w_mix, w_ab, l, tm=ns)
        ys, *st_s = _mix_sample(z, zab, cos_s, sin_s, gdn_conv_w, alog_v, dtb_v, gng3, sconv_w,
                                pool_w16, pscale3, state_ret, state_gdn, state_gdn_conv,
                                state_sconv, state_pool, l)
        hs = _out_proj(hs, norm_g3, ys, w_g, w_b, w_o, l, tm=ns)
        for i in range(5):
            new_p[i].append(st_p[i])
            new_s[i].append(st_s[i])

    fg = final_norm_g[None, :]
    y_prompt = _final_norm(hp, fg, 512).reshape(bp, tp, D_MODEL)
    y_sample = _final_norm(hs, fg, ns).reshape(ns, ts, D_MODEL)
    outs_p = [jnp.stack(a, axis=0) for a in new_p]
    outs_s = [jnp.stack(a, axis=0) for a in new_s]
    return (y_prompt, y_sample, *outs_p, *outs_s)
```

```python
import functools
import math

import jax
import jax.numpy as jnp
from jax import lax
from jax.experimental import pallas as pl
from jax.experimental.pallas import tpu as pltpu

F32 = jnp.float32
BF16 = jnp.bfloat16

D_MODEL = 2048
N_BRANCH = 4
BW = D_MODEL // N_BRANCH
HEADS = 4
HD = BW // HEADS
ROPE_BASE = 10000.0
GDN_CONV = 4
SCONV_W = 3
POOL_WINDOWS = (2, 4, 8, 16)
POOL_HIST = 15
GCHUNK = 64
EPS = 1e-6
PAST_LEN = 16384
N_MIX = 14 * BW
N_GATE = N_BRANCH * D_MODEL
LANES = 128
WCOL = 1024
VMEM_LIMIT = 48 * 1024 * 1024

C_RET_Q, C_RET_K, C_RET_V, C_RET_G = 0, BW, 2 * BW, 3 * BW
C_GDN_QKV, C_GDN_G = 4 * BW, 7 * BW
C_SC_B, C_SC_C, C_SC_H, C_SC_G = 8 * BW, 9 * BW, 10 * BW, 11 * BW
C_POOL_U, C_POOL_G = 12 * BW, 13 * BW


def _dot(a, b):
    return jnp.dot(a, b, preferred_element_type=F32)


def _dot_nt(a, b):
    return lax.dot_general(a, b, (((1,), (1,)), ((), ())), preferred_element_type=F32)


def _dot_tn(a, b):
    return lax.dot_general(a, b, (((0,), (0,)), ((), ())), preferred_element_type=F32)


def _split(x):
    hi = x.astype(BF16)
    return hi, (x - hi.astype(F32)).astype(BF16)


def _silu(x):
    return x * jax.nn.sigmoid(x)


def _softplus(x):
    return jnp.maximum(x, 0.0) + jnp.log1p(jnp.exp(-jnp.abs(x)))


def _rms_rows(x):
    return x * lax.rsqrt(jnp.mean(x * x, axis=-1, keepdims=True) + EPS)


def _rope(x, cos, sin_signed):
    return x * cos + pltpu.roll(x, HD // 2, axis=1) * sin_signed


def _log_gamma(h):
    return math.log(1.0 - 2.0 ** (-5.0 - h))


def _in_proj_kernel(x_ref, g_ref, w_ref, wab_ref, z_ref, zab_ref, h_ref):
    @pl.when(pl.program_id(1) == 0)
    def _():
        hb = (_rms_rows(x_ref[...]) * g_ref[...]).astype(BF16)
        h_ref[...] = hb
        zab_ref[...] = _dot(hb, wab_ref[...])

    z_ref[...] = _dot(h_ref[...], w_ref[...])


def _in_proj(x, norm_g, w_mix, w_ab, layer, tm):
    m = x.shape[0]
    return pl.pallas_call(
        _in_proj_kernel,
        grid=(m // tm, N_MIX // WCOL),
        in_specs=[
            pl.BlockSpec((tm, D_MODEL), lambda i, j: (i, 0)),
            pl.BlockSpec((None, 1, D_MODEL), lambda i, j: (layer, 0, 0)),
            pl.BlockSpec((None, D_MODEL, WCOL), lambda i, j: (layer, 0, j)),
            pl.BlockSpec((None, D_MODEL, LANES), lambda i, j: (layer, 0, 0)),
        ],
        out_specs=[
            pl.BlockSpec((tm, WCOL), lambda i, j: (i, j)),
            pl.BlockSpec((tm, LANES), lambda i, j: (i, 0)),
        ],
        out_shape=[
            jax.ShapeDtypeStruct((m, N_MIX), F32),
            jax.ShapeDtypeStruct((m, LANES), F32),
        ],
        scratch_shapes=[pltpu.VMEM((tm, D_MODEL), BF16)],
        compiler_params=pltpu.CompilerParams(
            dimension_semantics=("arbitrary", "arbitrary"), vmem_limit_bytes=VMEM_LIMIT),
        name="in_proj",
    )(x, norm_g, w_mix, w_ab)


def _packed_products(lhs_f32, m_hi_bd, m_lo_bd):
    parts = [_split(x) for x in lhs_f32]
    stack_all = jnp.concatenate([p for hl in parts for p in hl], axis=0)
    stack_hi = jnp.concatenate([hl[0] for hl in parts], axis=0)
    r1 = _dot(stack_all, m_hi_bd)
    r2 = _dot(stack_hi, m_lo_bd)
    out = []
    for i in range(len(lhs_f32)):
        a = 2 * i * GCHUNK
        b = i * GCHUNK
        out.append(r1[a:a + GCHUNK] + r1[a + GCHUNK:a + 2 * GCHUNK] + r2[b:b + GCHUNK])
    return out


def _block_diag(x16, bd_mask16):
    return jnp.concatenate([x16] * HEADS, axis=0) * bd_mask16


def _neumann_inverse_packed(l_p, eye_p, bd_mask16):
    bd = lambda m: tuple(_block_diag(x, bd_mask16) for x in _split(m))
    p = eye_p - l_p
    (m,) = _packed_products([l_p], *bd(l_p))
    steps = int(math.log2(GCHUNK)) - 1
    for s in range(steps):
        if s + 1 < steps:
            pm, m = _packed_products([p, m], *bd(m))
        else:
            (pm,) = _packed_products([p], *bd(m))
        p = p + pm
    return p


def _mix_prompt_kernel(z_ref, zab_ref, cos_ref, sin_ref, convw_ref, alog_ref, dtb_ref, gng_ref,
                       scw_ref, poolw_ref, pscale_ref,
                       ys_ref, pret_ref, pgdn_ref, pgconv_ref, psconv_ref, ppool_ref,
                       sret, sgdn, extq, extsc, extpool, dmask_ref, *, tt):
    b = pl.program_id(0)
    t = pl.program_id(1)
    n_t = pl.num_programs(1)
    n_chunk = tt // GCHUNK

    @pl.when((b == 0) & (t == 0))
    def _():
        ii = lax.broadcasted_iota(jnp.int32, (tt, tt), 0)
        jj = lax.broadcasted_iota(jnp.int32, (tt, tt), 1)
        causal = ii >= jj
        dif = jnp.where(causal, ii - jj, 0).astype(F32)
        for h in range(HEADS):
            dmask_ref[h] = jnp.where(causal, jnp.exp(dif * _log_gamma(h)), 0.0)

    @pl.when(t == 0)
    def _():
        sret[...] = jnp.zeros_like(sret)
        sgdn[...] = jnp.zeros_like(sgdn)
        extq[0:8, :] = jnp.zeros((8, 3 * BW), F32)
        extsc[0:8, :] = jnp.zeros((8, BW), F32)
        extpool[0:16, :] = jnp.zeros((16, BW), F32)

    cos = cos_ref[...]
    sin = sin_ref[...]

    ipos = lax.broadcasted_iota(jnp.int32, (tt, 1), 0).astype(F32)
    for h in range(HEADS):
        lg = _log_gamma(h)
        hs = slice(h * HD, (h + 1) * HD)
        q_scale = jnp.exp((ipos + 1.0) * lg)
        k_scale = jnp.exp((tt - 1.0 - ipos) * lg)
        q = _rope(z_ref[:, C_RET_Q + h * HD:C_RET_Q + (h + 1) * HD], cos, sin)
        k = _rope(z_ref[:, C_RET_K + h * HD:C_RET_K + (h + 1) * HD], cos, sin) * (HD ** -0.5)
        vb = z_ref[:, C_RET_V + h * HD:C_RET_V + (h + 1) * HD].astype(BF16)
        scores = _dot_nt(q.astype(BF16), k.astype(BF16)) * dmask_ref[h]
        s0 = sret[h]
        o = _dot(scores.astype(BF16), vb) + _dot((q * q_scale).astype(BF16), s0.astype(BF16))
        kv = _dot_tn((k * k_scale).astype(BF16), vb)
        sret[h] = s0 * math.exp(tt * lg) + kv
        ys_ref[0, :, hs] = (_rms_rows(o) * _silu(z_ref[:, C_RET_G + h * HD:C_RET_G + (h + 1) * HD])
                            ).astype(ys_ref.dtype)

    extq[8:8 + tt, :] = z_ref[:, C_GDN_QKV:C_GDN_QKV + 3 * BW]
    conv = extq[5:5 + tt, :] * convw_ref[0:1, :]
    for kk in range(1, GDN_CONV):
        conv = conv + extq[5 + kk:5 + kk + tt, :] * convw_ref[kk:kk + 1, :]
    qkv = _silu(conv)
    extq[0:8, :] = extq[tt:tt + 8, :]

    zab = zab_ref[...]
    g_all = -jnp.exp(alog_ref[...]) * _softplus(zab + dtb_ref[...])
    beta_all = jax.nn.sigmoid(zab)
    rin = lax.broadcasted_iota(jnp.int32, (tt, LANES), 0) & (GCHUNK - 1)
    b_all = g_all
    sh = 1
    while sh < GCHUNK:
        b_all = b_all + jnp.where(rin >= sh, pltpu.roll(b_all, sh, axis=0), 0.0)
        sh *= 2
    b_all_t = b_all.T

    ci = lax.broadcasted_iota(jnp.int32, (GCHUNK, GCHUNK), 0)
    cj = lax.broadcasted_iota(jnp.int32, (GCHUNK, GCHUNK), 1)
    c_causal = ci >= cj
    c_strict = ci > cj
    pi = lax.broadcasted_iota(jnp.int32, (GCHUNK, HEADS * GCHUNK), 0)
    pj = lax.broadcasted_iota(jnp.int32, (GCHUNK, HEADS * GCHUNK), 1)
    eye_p = jnp.where(pi == (pj & (GCHUNK - 1)), 1.0, 0.0).astype(F32)
    bi = lax.broadcasted_iota(jnp.int32, (HEADS * GCHUNK, HEADS * GCHUNK), 0)
    bj = lax.broadcasted_iota(jnp.int32, (HEADS * GCHUNK, HEADS * GCHUNK), 1)
    bd_mask16 = jnp.where((bi ^ bj) < GCHUNK, 1.0, 0.0).astype(BF16)

    qs, ks, vs = [], [], []
    for h in range(HEADS):
        qh = qkv[:, h * HD:(h + 1) * HD]
        kh = qkv[:, BW + h * HD:BW + (h + 1) * HD]
        qs.append(qh * lax.rsqrt(jnp.sum(qh * qh, axis=-1, keepdims=True) + EPS) * (HD ** -0.5))
        ks.append(kh * lax.rsqrt(jnp.sum(kh * kh, axis=-1, keepdims=True) + EPS))
        vs.append(qkv[:, 2 * BW + h * HD:2 * BW + (h + 1) * HD])

    prep = []
    for c in range(n_chunk):
        rs = slice(c * GCHUNK, (c + 1) * GCHUNK)
        per_head, l_blocks = [], []
        for h in range(HEADS):
            q, k, v = qs[h][rs], ks[h][rs], vs[h][rs]
            bcol = b_all[rs, h:h + 1]
            brow = b_all_t[h:h + 1, rs]
            beta = beta_all[rs, HEADS + h:HEADS + h + 1]
            blast = b_all[c * GCHUNK + GCHUNK - 1:(c + 1) * GCHUNK, h:h + 1]
            eb = jnp.exp(bcol)
            decay = jnp.where(c_causal, jnp.exp(jnp.where(c_causal, bcol - brow, 0.0)), 0.0)
            kb = k * beta
            k16 = k.astype(BF16)
            kq = _dot_nt(jnp.concatenate([kb.astype(BF16), q.astype(BF16)], axis=0), k16)
            l_blocks.append(jnp.where(c_strict, kq[:GCHUNK] * decay, 0.0))
            attn16 = (kq[GCHUNK:] * decay).astype(BF16)
            rhs = jnp.concatenate([v * beta, kb * eb], axis=1)
            qe16 = (q * eb).astype(BF16)
            kdec16 = (k * jnp.exp(blast - bcol)).astype(BF16)
            per_head.append((rhs, attn16, qe16, kdec16, jnp.exp(blast)))
        t_p = _neumann_inverse_packed(jnp.concatenate(l_blocks, axis=1), eye_p, bd_mask16)
        sols = []
        for h in range(HEADS):
            rhs = per_head[h][0]
            t_hi, t_lo = _split(t_p[:, h * GCHUNK:(h + 1) * GCHUNK])
            r_hi, r_lo = _split(rhs)
            r1 = _dot(jnp.concatenate([t_hi, t_lo], axis=0), r_hi)
            sols.append(r1[:GCHUNK] + r1[GCHUNK:] + _dot(t_hi, r_lo))
        prep.append((per_head, sols))

    for h in range(HEADS):
        s = sgdn[h]
        o_chunks = []
        for c in range(n_chunk):
            _, attn16, qe16, kdec16, e_last = prep[c][0][h]
            sol = prep[c][1][h]
            u, w = sol[:, :HD], sol[:, HD:]
            ws_qs = _dot(jnp.concatenate([w.astype(BF16), qe16], axis=0), s.astype(BF16))
            v_new16 = (u - ws_qs[:GCHUNK]).astype(BF16)
            o_chunks.append(ws_qs[GCHUNK:] + _dot(attn16, v_new16))
            s = s * e_last + _dot_tn(kdec16, v_new16)
        sgdn[h] = s
        o = jnp.concatenate(o_chunks, axis=0)
        ys_ref[1, :, h * HD:(h + 1) * HD] = (
            _rms_rows(o) * gng_ref[...] * _silu(z_ref[:, C_GDN_G + h * HD:C_GDN_G + (h + 1) * HD])
        ).astype(ys_ref.dtype)

    extsc[8:8 + tt, :] = z_ref[:, C_SC_C:C_SC_C + BW] * z_ref[:, C_SC_H:C_SC_H + BW]
    conv = extsc[6:6 + tt, :] * scw_ref[0:1, :]
    for kk in range(1, SCONV_W):
        conv = conv + extsc[6 + kk:6 + kk + tt, :] * scw_ref[kk:kk + 1, :]
    ys_ref[2] = (z_ref[:, C_SC_B:C_SC_B + BW] * conv * _silu(z_ref[:, C_SC_G:C_SC_G + BW])
                 ).astype(ys_ref.dtype)
    extsc[0:8, :] = extsc[tt:tt + 8, :]

    extpool[16:16 + tt, :] = z_ref[:, C_POOL_U:C_POOL_U + BW]
    pos1 = (t * tt + lax.broadcasted_iota(jnp.int32, (tt, 1), 0) + 1).astype(F32)
    for gi, wnd in enumerate(POOL_WINDOWS):
        gs = slice(gi * HD, (gi + 1) * HD)
        acc = extpool[:, gs]
        span = 1
        while span < wnd:
            acc = acc + pltpu.roll(acc, span, axis=0)
            span *= 2
        cur = z_ref[:, C_POOL_U + gi * HD:C_POOL_U + (gi + 1) * HD]
        pooled = acc[16:] / jnp.minimum(pos1, float(wnd)) - cur
        mixed = _dot(pooled.astype(BF16), poolw_ref[gi])
        ys_ref[3, :, gs] = (mixed * pscale_ref[:, gs]
                            * _silu(z_ref[:, C_POOL_G + gi * HD:C_POOL_G + (gi + 1) * HD])
                            ).astype(ys_ref.dtype)
    extpool[0:16, :] = extpool[tt:tt + 16, :]

    @pl.when(t == n_t - 1)
    def _():
        pret_ref[...] = sret[...]
        pgdn_ref[...] = sgdn[...]
        pgconv_ref[...] = extq[8 - (GDN_CONV - 1):8, :]
        psconv_ref[...] = extsc[8 - (SCONV_W - 1):8, :]
        ppool_ref[...] = extpool[16 - POOL_HIST:16, :]


def _mix_prompt(z, zab, cos_t, sin_t, gdn_conv_w, alog_v, dtb_v, gdn_norm_g, sconv_w, pool_w16,
                pool_scale, layer, nb, nt, tt):
    full = lambda *shape: pl.BlockSpec((None,) + shape, lambda b, t: (layer,) + (0,) * len(shape))
    state = lambda *shape: pl.BlockSpec((None,) + shape, lambda b, t: (b,) + (0,) * len(shape))
    return pl.pallas_call(
        functools.partial(_mix_prompt_kernel, tt=tt),
        grid=(nb, nt),
        in_specs=[
            pl.BlockSpec((tt, N_MIX), lambda b, t: (b * nt + t, 0)),
            pl.BlockSpec((tt, LANES), lambda b, t: (b * nt + t, 0)),
            pl.BlockSpec((tt, HD), lambda b, t: (t, 0)),
            pl.BlockSpec((tt, HD), lambda b, t: (t, 0)),
            full(GDN_CONV, 3 * BW), full(1, LANES), full(1, LANES), full(1, HD),
            full(SCONV_W, BW), full(HEADS, HD, HD), full(1, BW),
        ],
        out_specs=[
            pl.BlockSpec((N_BRANCH, tt, BW), lambda b, t: (0, b * nt + t, 0)),
            state(HEADS, HD, HD), state(HEADS, HD, HD),
            state(GDN_CONV - 1, 3 * BW), state(SCONV_W - 1, BW), state(POOL_HIST, BW),
        ],
        out_shape=[
            jax.ShapeDtypeStruct((N_BRANCH, nb * nt * tt, BW), BF16),
            jax.ShapeDtypeStruct((nb, HEADS, HD, HD), F32),
            jax.ShapeDtypeStruct((nb, HEADS, HD, HD), F32),
            jax.ShapeDtypeStruct((nb, GDN_CONV - 1, 3 * BW), F32),
            jax.ShapeDtypeStruct((nb, SCONV_W - 1, BW), F32),
            jax.ShapeDtypeStruct((nb, POOL_HIST, BW), F32),
        ],
        scratch_shapes=[
            pltpu.VMEM((HEADS, HD, HD), F32),
            pltpu.VMEM((HEADS, HD, HD), F32),
            pltpu.VMEM((tt + 8, 3 * BW), F32),
            pltpu.VMEM((tt + 8, BW), F32),
            pltpu.VMEM((tt + 16, BW), F32),
            pltpu.VMEM((HEADS, tt, tt), F32),
        ],
        compiler_params=pltpu.CompilerParams(
            dimension_semantics=("arbitrary", "arbitrary"), vmem_limit_bytes=VMEM_LIMIT),
        name="mix_prompt",
    )(z, zab, cos_t, sin_t, gdn_conv_w, alog_v, dtb_v, gdn_norm_g, sconv_w, pool_w16, pool_scale)


SB = 8


def _mix_sample_kernel(z_ref, zab_ref, cos_ref, sin_ref, convw_ref, alog_ref, dtb_ref, gng_ref,
                       scw_ref, poolw_ref, pscale_ref,
                       sret_in, sgdn_in, gconv_in, sconv_in, pool_in, *rest):
    ys_ref, sret_out, sgdn_out, gconv_out, sconv_out, pool_out, o_scr = rest[-7:]
    cos = cos_ref[...]
    sin = sin_ref[...]
    zero_rows = jnp.zeros((HD - SB, HD), F32)

    def columns(x):
        return jnp.concatenate([x, zero_rows], axis=0).T

    for h in range(HEADS):
        gamma = math.exp(_log_gamma(h))
        q = _rope(z_ref[:, C_RET_Q + h * HD:C_RET_Q + (h + 1) * HD], cos, sin)
        k = _rope(z_ref[:, C_RET_K + h * HD:C_RET_K + (h + 1) * HD], cos, sin) * (HD ** -0.5)
        v = z_ref[:, C_RET_V + h * HD:C_RET_V + (h + 1) * HD]
        q_t, k_t = columns(q), columns(k)
        for j in range(SB):
            s_new = sret_in[j, h] * gamma + k_t[:, j:j + 1] * v[j:j + 1, :]
            sret_out[j, h] = s_new
            o_scr[j:j + 1, h * HD:(h + 1) * HD] = jnp.sum(q_t[:, j:j + 1] * s_new, axis=0,
                                                          keepdims=True)

    u_qkv = z_ref[:, C_GDN_QKV:C_GDN_QKV + 3 * BW]
    conv = u_qkv * convw_ref[GDN_CONV - 1:GDN_CONV, :]
    for kk in range(GDN_CONV - 1):
        conv = conv + gconv_in[:, kk, :] * convw_ref[kk:kk + 1, :]
    qkv = _silu(conv)
    for kk in range(GDN_CONV - 2):
        gconv_out[:, kk, :] = gconv_in[:, kk + 1, :]
    gconv_out[:, GDN_CONV - 2, :] = u_qkv

    zab = zab_ref[...]
    decay_all = jnp.exp(-jnp.exp(alog_ref[...]) * _softplus(zab + dtb_ref[...]))
    beta_all = jax.nn.sigmoid(zab)
    for h in range(HEADS):
        qh = qkv[:, h * HD:(h + 1) * HD]
        kh = qkv[:, BW + h * HD:BW + (h + 1) * HD]
        vh = qkv[:, 2 * BW + h * HD:2 * BW + (h + 1) * HD]
        qh = qh * lax.rsqrt(jnp.sum(qh * qh, axis=-1, keepdims=True) + EPS) * (HD ** -0.5)
        kh = kh * lax.rsqrt(jnp.sum(kh * kh, axis=-1, keepdims=True) + EPS)
        q_t, k_t = columns(qh), columns(kh)
        for j in range(SB):
            a = decay_all[j:j + 1, h:h + 1]
            beta = beta_all[j:j + 1, HEADS + h:HEADS + h + 1]
            s_dec = sgdn_in[j, h] * a
            k_col = k_t[:, j:j + 1]
            v_new = (vh[j:j + 1, :] - jnp.sum(k_col * s_dec, axis=0, keepdims=True)) * beta
            s_new = s_dec + k_col * v_new
            sgdn_out[j, h] = s_new
            o_scr[j:j + 1, BW + h * HD:BW + (h + 1) * HD] = jnp.sum(q_t[:, j:j + 1] * s_new,
                                                                    axis=0, keepdims=True)

    for h in range(HEADS):
        hs = slice(h * HD, (h + 1) * HD)
        ys_ref[0, :, hs] = (_rms_rows(o_scr[:, hs])
                            * _silu(z_ref[:, C_RET_G + h * HD:C_RET_G + (h + 1) * HD]))
        ys_ref[1, :, hs] = (_rms_rows(o_scr[:, BW + h * HD:BW + (h + 1) * HD]) * gng_ref[...]
                            * _silu(z_ref[:, C_GDN_G + h * HD:C_GDN_G + (h + 1) * HD]))

    prod = z_ref[:, C_SC_C:C_SC_C + BW] * z_ref[:, C_SC_H:C_SC_H + BW]
    conv = prod * scw_ref[SCONV_W - 1:SCONV_W, :]
    for kk in range(SCONV_W - 1):
        conv = conv + sconv_in[:, kk, :] * scw_ref[kk:kk + 1, :]
    ys_ref[2] = z_ref[:, C_SC_B:C_SC_B + BW] * conv * _silu(z_ref[:, C_SC_G:C_SC_G + BW])
    for kk in range(SCONV_W - 2):
        sconv_out[:, kk, :] = sconv_in[:, kk + 1, :]
    sconv_out[:, SCONV_W - 2, :] = prod

    pool_u = z_ref[:, C_POOL_U:C_POOL_U + BW]
    for gi, wnd in enumerate(POOL_WINDOWS):
        gs = slice(gi * HD, (gi + 1) * HD)
        cur = pool_u[:, gs]
        acc = cur
        for d in range(1, wnd):
            acc = acc + pool_in[:, POOL_HIST - d, gs]
        pooled = acc / float(min(PAST_LEN + 1, wnd)) - cur
        mixed = _dot(pooled.astype(BF16), poolw_ref[gi])
        ys_ref[3, :, gs] = (mixed * pscale_ref[:, gs]
                            * _silu(z_ref[:, C_POOL_G + gi * HD:C_POOL_G + (gi + 1) * HD]))
    for r in range(POOL_HIST - 1):
        pool_out[:, r, :] = pool_in[:, r + 1, :]
    pool_out[:, POOL_HIST - 1, :] = pool_u


def _mix_sample(z, zab, cos_t, sin_t, gdn_conv_w, alog_v, dtb_v, gdn_norm_g, sconv_w, pool_w16,
                pool_scale, state_ret, state_gdn, state_gdn_conv, state_sconv, state_pool, layer,
                big_states):
    ns = z.shape[0]
    depth = state_ret.shape[0]
    full = lambda *shape: pl.BlockSpec((None,) + shape, lambda i: (layer,) + (0,) * len(shape))
    st_in = lambda *shape: pl.BlockSpec((None, SB) + shape, lambda i: (layer, i) + (0,) * len(shape))
    st_out = lambda *shape: pl.BlockSpec((SB,) + shape, lambda i: (i,) + (0,) * len(shape))
    big_shape = jax.ShapeDtypeStruct((depth, ns, HEADS, HD, HD), F32)
    n_fixed = 16
    aliased = [] if big_states is None else list(big_states)
    return pl.pallas_call(
        _mix_sample_kernel,
        grid=(ns // SB,),
        in_specs=[
            pl.BlockSpec((SB, N_MIX), lambda i: (i, 0)),
            pl.BlockSpec((SB, LANES), lambda i: (i, 0)),
            pl.BlockSpec((1, HD), lambda i: (0, 0)),
            pl.BlockSpec((1, HD), lambda i: (0, 0)),
            full(GDN_CONV, 3 * BW), full(1, LANES), full(1, LANES), full(1, HD),
            full(SCONV_W, BW), full(HEADS, HD, HD), full(1, BW),
            st_in(HEADS, HD, HD), st_in(HEADS, HD, HD),
            st_in(GDN_CONV - 1, 3 * BW), st_in(SCONV_W - 1, BW), st_in(POOL_HIST, BW),
        ] + [pl.BlockSpec(memory_space=pl.ANY)] * len(aliased),
        out_specs=[
            pl.BlockSpec((N_BRANCH, SB, BW), lambda i: (0, i, 0)),
            st_in(HEADS, HD, HD), st_in(HEADS, HD, HD),
            st_out(GDN_CONV - 1, 3 * BW), st_out(SCONV_W - 1, BW), st_out(POOL_HIST, BW),
        ],
        out_shape=[
            jax.ShapeDtypeStruct((N_BRANCH, ns, BW), F32),
            big_shape, big_shape,
            jax.ShapeDtypeStruct((ns, GDN_CONV - 1, 3 * BW), F32),
            jax.ShapeDtypeStruct((ns, SCONV_W - 1, BW), F32),
            jax.ShapeDtypeStruct((ns, POOL_HIST, BW), F32),
        ],
        input_output_aliases={n_fixed + i: 1 + i for i in range(len(aliased))},
        scratch_shapes=[pltpu.VMEM((SB, 2 * BW), F32)],
        compiler_params=pltpu.CompilerParams(
            dimension_semantics=("arbitrary",), vmem_limit_bytes=VMEM_LIMIT),
        name="mix_sample",
    )(z, zab, cos_t, sin_t, gdn_conv_w, alog_v, dtb_v, gdn_norm_g, sconv_w, pool_w16, pool_scale,
      state_ret, state_gdn, state_gdn_conv, state_sconv, state_pool, *aliased)


N_CT = D_MODEL // WCOL
N_GATE_PHASES = N_CT * N_BRANCH
N_OUT_PHASES = N_GATE_PHASES + N_CT


def _out_kernel(x_ref, g_ref, ys_ref, w_ref, wb_ref, xc_ref, o_ref, h_ref, acc_ref, merged_ref):
    p = pl.program_id(1)
    n = p % N_BRANCH

    @pl.when(p == 0)
    def _():
        h_ref[...] = (_rms_rows(x_ref[...]) * g_ref[...]).astype(BF16)

    @pl.when(p < N_GATE_PHASES)
    def _():
        gate = jax.nn.sigmoid(_dot(h_ref[...], w_ref[...]))
        contrib = gate * _dot(ys_ref[...].astype(BF16), wb_ref[...])

        @pl.when(n == 0)
        def _():
            acc_ref[...] = contrib

        @pl.when((n > 0) & (n < N_BRANCH - 1))
        def _():
            acc_ref[...] += contrib

        @pl.when(n == N_BRANCH - 1)
        def _():
            merged_ref[p // N_BRANCH] = (acc_ref[...] + contrib).astype(BF16)

    @pl.when(p >= N_GATE_PHASES)
    def _():
        acc = xc_ref[...]
        for c in range(N_CT):
            acc = acc + _dot(merged_ref[c], w_ref[c * WCOL:(c + 1) * WCOL, :])
        o_ref[...] = acc


def _out_proj(x, norm_g, ys, w_go, w_b, layer, tm):
    m = x.shape[0]
    gate_phase = lambda p: jnp.minimum(p, N_GATE_PHASES - 1)
    out_tile = lambda p: jnp.maximum(p - N_GATE_PHASES, 0)
    return pl.pallas_call(
        _out_kernel,
        grid=(m // tm, N_OUT_PHASES),
        in_specs=[
            pl.BlockSpec((tm, D_MODEL), lambda i, p: (i, 0)),
            pl.BlockSpec((None, 1, D_MODEL), lambda i, p: (layer, 0, 0)),
            pl.BlockSpec((None, tm, BW), lambda i, p: (gate_phase(p) % N_BRANCH, i, 0)),
            pl.BlockSpec((None, D_MODEL, WCOL), lambda i, p: (layer, 0, p)),
            pl.BlockSpec((None, None, BW, WCOL),
                         lambda i, p: (layer, gate_phase(p) % N_BRANCH, 0, gate_phase(p) // N_BRANCH)),
            pl.BlockSpec((tm, WCOL), lambda i, p: (i, out_tile(p))),
        ],
        out_specs=pl.BlockSpec((tm, WCOL), lambda i, p: (i, out_tile(p))),
        out_shape=jax.ShapeDtypeStruct((m, D_MODEL), F32),
        scratch_shapes=[pltpu.VMEM((tm, D_MODEL), BF16), pltpu.VMEM((tm, WCOL), F32),
                        pltpu.VMEM((N_CT, tm, WCOL), BF16)],
        compiler_params=pltpu.CompilerParams(
            dimension_semantics=("arbitrary", "arbitrary"), vmem_limit_bytes=VMEM_LIMIT),
        name="out_proj",
    )(x, norm_g, ys, w_go, w_b, x)


def _final_norm_kernel(x_ref, g_ref, o_ref):
    o_ref[...] = _rms_rows(x_ref[...]) * g_ref[...]


def _final_norm(x, g, tm):
    m = x.shape[0]
    return pl.pallas_call(
        _final_norm_kernel,
        grid=(m // tm,),
        in_specs=[pl.BlockSpec((tm, D_MODEL), lambda i: (i, 0)),
                  pl.BlockSpec((1, D_MODEL), lambda i: (0, 0))],
        out_specs=pl.BlockSpec((tm, D_MODEL), lambda i: (i, 0)),
        out_shape=jax.ShapeDtypeStruct((m, D_MODEL), F32),
        compiler_params=pltpu.CompilerParams(dimension_semantics=("arbitrary",)),
        name="final_norm",
    )(x, g)


def _rope_tables(pos):
    half = HD // 2
    inv = ROPE_BASE ** (-jnp.arange(half, dtype=F32) / half)
    ang = pos.astype(F32)[:, None] * inv[None, :]
    cos, sin = jnp.cos(ang), jnp.sin(ang)
    return jnp.concatenate([cos, cos], axis=1), jnp.concatenate([-sin, sin], axis=1)


def _prepare_weights(w_in, w_out):
    ab0 = 8 * BW
    g0 = N_MIX + 2 * HEADS
    w_mix = jnp.concatenate([w_in[:, :, :ab0], w_in[:, :, ab0 + 2 * HEADS:g0]], axis=2).astype(BF16)
    w_ab = jnp.pad(w_in[:, :, ab0:ab0 + 2 * HEADS],
                   ((0, 0), (0, 0), (0, LANES - 2 * HEADS))).astype(BF16)
    pieces = [w_in[:, :, g0 + n * D_MODEL + c * WCOL:g0 + n * D_MODEL + (c + 1) * WCOL]
              for c in range(N_CT) for n in range(N_BRANCH)]
    w_go = jnp.concatenate(pieces + [w_out], axis=2).astype(BF16)
    return w_mix, w_ab, w_go


def kernel(x_prompt, x_sample, state_ret, state_gdn, state_gdn_conv, state_sconv, state_pool,
           norm_g, w_in, gdn_conv_w, gdn_a_log, gdn_dt_bias, gdn_norm_g, sconv_w, pool_w, pool_scale,
           w_branch, w_out, final_norm_g):
    bp, tp, _ = x_prompt.shape
    ns, ts, _ = x_sample.shape
    depth = w_in.shape[0]
    tt = min(256, tp)
    nt = tp // tt
    tm_in = min(1024, bp * tp)
    tm_out = min(512, bp * tp)

    w_mix, w_ab, w_go = _prepare_weights(w_in, w_out)
    w_b = w_branch.astype(BF16)
    pool_w16 = pool_w.astype(BF16)
    pad_h = lambda a: jnp.pad(a, ((0, 0), (0, LANES - HEADS)))[:, None, :]
    alog_v, dtb_v = pad_h(gdn_a_log), pad_h(gdn_dt_bias)
    norm_g3 = norm_g[:, None, :]
    gng3 = gdn_norm_g[:, None, :]
    pscale3 = pool_scale[:, None, :]

    cos_p, sin_p = _rope_tables(jnp.arange(tp, dtype=jnp.int32))
    cos_s, sin_s = _rope_tables(PAST_LEN + jnp.arange(ts, dtype=jnp.int32))

    hp = x_prompt.reshape(bp * tp, D_MODEL)
    hs = x_sample.reshape(ns * ts, D_MODEL)
    new_p = [[] for _ in range(5)]
    new_s = [[] for _ in range(3)]
    big_states = None
    for l in range(depth):
        z, zab = _in_proj(hp, norm_g3, w_mix, w_ab, l, tm=tm_in)
        ys, *st_p = _mix_prompt(z, zab, cos_p, sin_p, gdn_conv_w, alog_v, dtb_v, gng3, sconv_w,
                                pool_w16, pscale3, l, bp, nt, tt)
        hp = _out_proj(hp, norm_g3, ys, w_go, w_b, l, tm=tm_out)

        z, zab = _in_proj(hs, norm_g3, w_mix, w_ab, l, tm=ns)
        ys, s_ret, s_gdn, *st_s = _mix_sample(
            z, zab, cos_s, sin_s, gdn_conv_w, alog_v, dtb_v, gng3, sconv_w, pool_w16, pscale3,
            state_ret, state_gdn, state_gdn_conv, state_sconv, state_pool, l, big_states)
        big_states = (s_ret, s_gdn)
        hs = _out_proj(hs, norm_g3, ys, w_go, w_b, l, tm=ns)
        for i in range(5):
            new_p[i].append(st_p[i])
        for i in range(3):
            new_s[i].append(st_s[i])

    fg = final_norm_g[None, :]
    y_prompt = _final_norm(hp, fg, 512).reshape(bp, tp, D_MODEL)
    y_sample = _final_norm(hs, fg, ns).reshape(ns, ts, D_MODEL)
    outs_p = [jnp.stack(a, axis=0) for a in new_p]
    outs_s = [jnp.stack(a, axis=0) for a in new_s]
    return (y_prompt, y_sample, *outs_p, *big_states, *outs_s)
```

```python
import functools
import math

import jax
import jax.numpy as jnp
from jax import lax
from jax.experimental import pallas as pl
from jax.experimental.pallas import tpu as pltpu

F32 = jnp.float32
BF16 = jnp.bfloat16

D_MODEL = 2048
N_BRANCH = 4
BW = D_MODEL // N_BRANCH
HEADS = 4
HD = BW // HEADS
ROPE_BASE = 10000.0
GDN_CONV = 4
SCONV_W = 3
POOL_WINDOWS = (2, 4, 8, 16)
POOL_HIST = 15
GCHUNK = 64
EPS = 1e-6
PAST_LEN = 16384
N_MIX = 14 * BW
N_GATE = N_BRANCH * D_MODEL
LANES = 128
WCOL = 1024
VMEM_LIMIT = 56 * 1024 * 1024

C_RET_Q, C_RET_K, C_RET_V, C_RET_G = 0, BW, 2 * BW, 3 * BW
C_GDN_QKV, C_GDN_G = 4 * BW, 7 * BW
C_SC_B, C_SC_C, C_SC_H, C_SC_G = 8 * BW, 9 * BW, 10 * BW, 11 * BW
C_POOL_U, C_POOL_G = 12 * BW, 13 * BW


def _dot(a, b):
    return jnp.dot(a, b, preferred_element_type=F32)


def _dot_nt(a, b):
    return lax.dot_general(a, b, (((1,), (1,)), ((), ())), preferred_element_type=F32)


def _dot_tn(a, b):
    return lax.dot_general(a, b, (((0,), (0,)), ((), ())), preferred_element_type=F32)


def _split(x):
    hi = x.astype(BF16)
    return hi, (x - hi.astype(F32)).astype(BF16)


def _sigmoid(x):
    return 0.5 * jnp.tanh(0.5 * x) + 0.5


def _silu(x):
    return x * _sigmoid(x)


def _softplus(x):
    return jnp.maximum(x, 0.0) + jnp.log1p(jnp.exp(-jnp.abs(x)))


def _rms_rows(x):
    return x * lax.rsqrt(jnp.mean(x * x, axis=-1, keepdims=True) + EPS)


def _rope(x, cos, sin_signed):
    return x * cos + pltpu.roll(x, HD // 2, axis=1) * sin_signed


def _log_gamma(h):
    return math.log(1.0 - 2.0 ** (-5.0 - h))


N_A_BLOCKS = (8 * BW) // WCOL
N_MIX_BLOCKS = N_MIX // WCOL


def _in_proj_kernel(x_ref, g_ref, wa_ref, wb_ref, wab_ref, z_ref, zab_ref, h_ref):
    j = pl.program_id(1)

    @pl.when(j == 0)
    def _():
        hb = (_rms_rows(x_ref[...]) * g_ref[...]).astype(BF16)
        h_ref[...] = hb
        zab_ref[...] = _dot(hb, wab_ref[...])

    @pl.when(j < N_A_BLOCKS)
    def _():
        z_ref[...] = _dot(h_ref[...], wa_ref[...])

    @pl.when(j >= N_A_BLOCKS)
    def _():
        z_ref[...] = _dot(h_ref[...], wb_ref[...])


def _in_proj(x, norm_g, w_a, w_b, w_ab, layer, tm):
    m = x.shape[0]
    return pl.pallas_call(
        _in_proj_kernel,
        grid=(m // tm, N_MIX_BLOCKS),
        in_specs=[
            pl.BlockSpec((tm, D_MODEL), lambda i, j: (i, 0)),
            pl.BlockSpec((None, 1, D_MODEL), lambda i, j: (layer, 0, 0)),
            pl.BlockSpec((None, D_MODEL, WCOL),
                         lambda i, j: (layer, 0, jnp.minimum(j, N_A_BLOCKS - 1))),
            pl.BlockSpec((None, D_MODEL, WCOL),
                         lambda i, j: (layer, 0, jnp.maximum(j - N_A_BLOCKS, 0))),
            pl.BlockSpec((None, D_MODEL, LANES), lambda i, j: (layer, 0, 0)),
        ],
        out_specs=[
            pl.BlockSpec((tm, WCOL), lambda i, j: (i, j)),
            pl.BlockSpec((tm, LANES), lambda i, j: (i, 0)),
        ],
        out_shape=[
            jax.ShapeDtypeStruct((m, N_MIX), F32),
            jax.ShapeDtypeStruct((m, LANES), F32),
        ],
        scratch_shapes=[pltpu.VMEM((tm, D_MODEL), BF16)],
        compiler_params=pltpu.CompilerParams(
            dimension_semantics=("arbitrary", "arbitrary"), vmem_limit_bytes=VMEM_LIMIT),
        name="in_proj",
    )(x, norm_g, w_a, w_b, w_ab)


def _packed_products(lhs_f32, m_hi_bd, m_lo_bd):
    parts = [_split(x) for x in lhs_f32]
    stack_all = jnp.concatenate([p for hl in parts for p in hl], axis=0)
    stack_hi = jnp.concatenate([hl[0] for hl in parts], axis=0)
    r1 = _dot(stack_all, m_hi_bd)
    r2 = _dot(stack_hi, m_lo_bd)
    out = []
    for i in range(len(lhs_f32)):
        a = 2 * i * GCHUNK
        b = i * GCHUNK
        out.append(r1[a:a + GCHUNK] + r1[a + GCHUNK:a + 2 * GCHUNK] + r2[b:b + GCHUNK])
    return out


def _block_diag(x16, bd_mask16):
    return jnp.concatenate([x16] * HEADS, axis=0) * bd_mask16


def _neumann_inverse_packed(l_ps, eye_p, bd_mask16):
    bd = lambda m: tuple(_block_diag(x, bd_mask16) for x in _split(m))
    ps = [eye_p - l_p for l_p in l_ps]
    ms = [_packed_products([l_p], *bd(l_p))[0] for l_p in l_ps]
    steps = int(math.log2(GCHUNK)) - 1
    for s in range(steps):
        for c in range(len(l_ps)):
            if s + 1 < steps:
                pm, ms[c] = _packed_products([ps[c], ms[c]], *bd(ms[c]))
            else:
                (pm,) = _packed_products([ps[c]], *bd(ms[c]))
            ps[c] = ps[c] + pm
    return ps


def _mix_prompt_kernel(z_ref, zab_ref, cos_ref, sin_ref, convw_ref, alog_ref, dtb_ref, gng_ref,
                       scw_ref, poolw_ref, pscale_ref,
                       ys_ref, pret_ref, pgdn_ref, pgconv_ref, psconv_ref, ppool_ref,
                       sret, sgdn, extq, extsc, extpool, dmask_ref, *, tt):
    b = pl.program_id(0)
    t = pl.program_id(1)
    n_t = pl.num_programs(1)
    n_chunk = tt // GCHUNK

    @pl.when((b == 0) & (t == 0))
    def _():
        ii = lax.broadcasted_iota(jnp.int32, (tt, tt), 0)
        jj = lax.broadcasted_iota(jnp.int32, (tt, tt), 1)
        causal = ii >= jj
        dif = jnp.where(causal, ii - jj, 0).astype(F32)
        for h in range(HEADS):
            dmask_ref[h] = jnp.where(causal, jnp.exp(dif * _log_gamma(h)), 0.0)

    @pl.when(t == 0)
    def _():
        sret[...] = jnp.zeros_like(sret)
        sgdn[...] = jnp.zeros_like(sgdn)
        extq[0:8, :] = jnp.zeros((8, 3 * BW), F32)
        extsc[0:8, :] = jnp.zeros((8, BW), F32)
        extpool[0:16, :] = jnp.zeros((16, BW), F32)

    cos = cos_ref[...]
    sin = sin_ref[...]

    ipos = lax.broadcasted_iota(jnp.int32, (tt, 1), 0).astype(F32)

    def retention_head(h):
        lg = _log_gamma(h)
        hs = slice(h * HD, (h + 1) * HD)
        q_scale = jnp.exp((ipos + 1.0) * lg)
        k_scale = jnp.exp((tt - 1.0 - ipos) * lg)
        q = _rope(z_ref[:, C_RET_Q + h * HD:C_RET_Q + (h + 1) * HD], cos, sin)
        k = _rope(z_ref[:, C_RET_K + h * HD:C_RET_K + (h + 1) * HD], cos, sin) * (HD ** -0.5)
        vb = z_ref[:, C_RET_V + h * HD:C_RET_V + (h + 1) * HD].astype(BF16)
        scores = _dot_nt(q.astype(BF16), k.astype(BF16)) * dmask_ref[h]
        s0 = sret[h]
        o = _dot(scores.astype(BF16), vb) + _dot((q * q_scale).astype(BF16), s0.astype(BF16))
        kv = _dot_tn((k * k_scale).astype(BF16), vb)
        sret[h] = s0 * math.exp(tt * lg) + kv
        ys_ref[0, :, hs] = (_rms_rows(o) * _silu(z_ref[:, C_RET_G + h * HD:C_RET_G + (h + 1) * HD])
                            ).astype(ys_ref.dtype)

    extq[8:8 + tt, :] = z_ref[:, C_GDN_QKV:C_GDN_QKV + 3 * BW]
    conv = extq[5:5 + tt, :] * convw_ref[0:1, :]
    for kk in range(1, GDN_CONV):
        conv = conv + extq[5 + kk:5 + kk + tt, :] * convw_ref[kk:kk + 1, :]
    qkv = _silu(conv)
    extq[0:8, :] = extq[tt:tt + 8, :]

    zab = zab_ref[...]
    g_all = -jnp.exp(alog_ref[...]) * _softplus(zab + dtb_ref[...])
    beta_all = _sigmoid(zab)
    rin = lax.broadcasted_iota(jnp.int32, (tt, LANES), 0) & (GCHUNK - 1)
    b_all = g_all
    sh = 1
    while sh < GCHUNK:
        b_all = b_all + jnp.where(rin >= sh, pltpu.roll(b_all, sh, axis=0), 0.0)
        sh *= 2
    b_all_t = b_all.T

    ci = lax.broadcasted_iota(jnp.int32, (GCHUNK, GCHUNK), 0)
    cj = lax.broadcasted_iota(jnp.int32, (GCHUNK, GCHUNK), 1)
    c_causal = ci >= cj
    c_strict = ci > cj
    pi = lax.broadcasted_iota(jnp.int32, (GCHUNK, HEADS * GCHUNK), 0)
    pj = lax.broadcasted_iota(jnp.int32, (GCHUNK, HEADS * GCHUNK), 1)
    eye_p = jnp.where(pi == (pj & (GCHUNK - 1)), 1.0, 0.0).astype(F32)
    bi = lax.broadcasted_iota(jnp.int32, (HEADS * GCHUNK, HEADS * GCHUNK), 0)
    bj = lax.broadcasted_iota(jnp.int32, (HEADS * GCHUNK, HEADS * GCHUNK), 1)
    bd_mask16 = jnp.where((bi ^ bj) < GCHUNK, 1.0, 0.0).astype(BF16)

    qs, ks, vs = [], [], []
    for h in range(HEADS):
        qh = qkv[:, h * HD:(h + 1) * HD]
        kh = qkv[:, BW + h * HD:BW + (h + 1) * HD]
        qs.append(qh * lax.rsqrt(jnp.sum(qh * qh, axis=-1, keepdims=True) + EPS) * (HD ** -0.5))
        ks.append(kh * lax.rsqrt(jnp.sum(kh * kh, axis=-1, keepdims=True) + EPS))
        vs.append(qkv[:, 2 * BW + h * HD:2 * BW + (h + 1) * HD])

    prep, l_ps = [], []
    for c in range(n_chunk):
        rs = slice(c * GCHUNK, (c + 1) * GCHUNK)
        per_head, l_blocks = [], []
        for h in range(HEADS):
            q, k, v = qs[h][rs], ks[h][rs], vs[h][rs]
            bcol = b_all[rs, h:h + 1]
            brow = b_all_t[h:h + 1, rs]
            beta = beta_all[rs, HEADS + h:HEADS + h + 1]
            blast = b_all[c * GCHUNK + GCHUNK - 1:(c + 1) * GCHUNK, h:h + 1]
            eb = jnp.exp(bcol)
            decay = jnp.where(c_causal, jnp.exp(jnp.where(c_causal, bcol - brow, 0.0)), 0.0)
            kb = k * beta
            k16 = k.astype(BF16)
            kq = _dot_nt(jnp.concatenate([kb.astype(BF16), q.astype(BF16)], axis=0), k16)
            l_blocks.append(jnp.where(c_strict, kq[:GCHUNK] * decay, 0.0))
            attn16 = (kq[GCHUNK:] * decay).astype(BF16)
            rhs = jnp.concatenate([v * beta, kb * eb], axis=1)
            qe16 = (q * eb).astype(BF16)
            kdec16 = (k * jnp.exp(blast - bcol)).astype(BF16)
            per_head.append((rhs, attn16, qe16, kdec16, jnp.exp(blast)))
        prep.append(per_head)
        l_ps.append(jnp.concatenate(l_blocks, axis=1))
    t_ps = _neumann_inverse_packed(l_ps, eye_p, bd_mask16)
    sols = []
    for c in range(n_chunk):
        sols_c = []
        for h in range(HEADS):
            t_hi, t_lo = _split(t_ps[c][:, h * GCHUNK:(h + 1) * GCHUNK])
            r_hi, r_lo = _split(prep[c][h][0])
            r1 = _dot(jnp.concatenate([t_hi, t_lo], axis=0), r_hi)
            sols_c.append(r1[:GCHUNK] + r1[GCHUNK:] + _dot(t_hi, r_lo))
        sols.append(sols_c)

    states = [sgdn[h] for h in range(HEADS)]
    o_chunks = [[] for _ in range(HEADS)]
    ret_heads = list(range(HEADS))
    for c in range(n_chunk):
        for h in range(HEADS):
            _, attn16, qe16, kdec16, e_last = prep[c][h]
            u, w = sols[c][h][:, :HD], sols[c][h][:, HD:]
            ws_qs = _dot(jnp.concatenate([w.astype(BF16), qe16], axis=0), states[h].astype(BF16))
            v_new16 = (u - ws_qs[:GCHUNK]).astype(BF16)
            o_chunks[h].append(ws_qs[GCHUNK:] + _dot(attn16, v_new16))
            states[h] = states[h] * e_last + _dot_tn(kdec16, v_new16)
        for h in ret_heads[c::n_chunk]:
            retention_head(h)
    for h in range(HEADS):
        sgdn[h] = states[h]
        o = jnp.concatenate(o_chunks[h], axis=0)
        ys_ref[1, :, h * HD:(h + 1) * HD] = (
            _rms_rows(o) * gng_ref[...] * _silu(z_ref[:, C_GDN_G + h * HD:C_GDN_G + (h + 1) * HD])
        ).astype(ys_ref.dtype)

    extsc[8:8 + tt, :] = z_ref[:, C_SC_C:C_SC_C + BW] * z_ref[:, C_SC_H:C_SC_H + BW]
    conv = extsc[6:6 + tt, :] * scw_ref[0:1, :]
    for kk in range(1, SCONV_W):
        conv = conv + extsc[6 + kk:6 + kk + tt, :] * scw_ref[kk:kk + 1, :]
    ys_ref[2] = (z_ref[:, C_SC_B:C_SC_B + BW] * conv * _silu(z_ref[:, C_SC_G:C_SC_G + BW])
                 ).astype(ys_ref.dtype)
    extsc[0:8, :] = extsc[tt:tt + 8, :]

    extpool[16:16 + tt, :] = z_ref[:, C_POOL_U:C_POOL_U + BW]
    pos1 = (t * tt + lax.broadcasted_iota(jnp.int32, (tt, 1), 0) + 1).astype(F32)
    for gi, wnd in enumerate(POOL_WINDOWS):
        gs = slice(gi * HD, (gi + 1) * HD)
        acc = extpool[:, gs]
        span = 1
        while span < wnd:
            acc = acc + pltpu.roll(acc, span, axis=0)
            span *= 2
        cur = z_ref[:, C_POOL_U + gi * HD:C_POOL_U + (gi + 1) * HD]
        pooled = acc[16:] / jnp.minimum(pos1, float(wnd)) - cur
        mixed = _dot(pooled.astype(BF16), poolw_ref[gi])
        ys_ref[3, :, gs] = (mixed * pscale_ref[:, gs]
                            * _silu(z_ref[:, C_POOL_G + gi * HD:C_POOL_G + (gi + 1) * HD])
                            ).astype(ys_ref.dtype)
    extpool[0:16, :] = extpool[tt:tt + 16, :]

    @pl.when(t == n_t - 1)
    def _():
        pret_ref[...] = sret[...]
        pgdn_ref[...] = sgdn[...]
        pgconv_ref[...] = extq[8 - (GDN_CONV - 1):8, :]
        psconv_ref[...] = extsc[8 - (SCONV_W - 1):8, :]
        ppool_ref[...] = extpool[16 - POOL_HIST:16, :]


def _mix_prompt(z, zab, cos_t, sin_t, gdn_conv_w, alog_v, dtb_v, gdn_norm_g, sconv_w, pool_w16,
                pool_scale, layer, nb, nt, tt):
    full = lambda *shape: pl.BlockSpec((None,) + shape, lambda b, t: (layer,) + (0,) * len(shape))
    state = lambda *shape: pl.BlockSpec((None,) + shape, lambda b, t: (b,) + (0,) * len(shape))
    return pl.pallas_call(
        functools.partial(_mix_prompt_kernel, tt=tt),
        grid=(nb, nt),
        in_specs=[
            pl.BlockSpec((tt, N_MIX), lambda b, t: (b * nt + t, 0)),
            pl.BlockSpec((tt, LANES), lambda b, t: (b * nt + t, 0)),
            pl.BlockSpec((tt, HD), lambda b, t: (t, 0)),
            pl.BlockSpec((tt, HD), lambda b, t: (t, 0)),
            full(GDN_CONV, 3 * BW), full(1, LANES), full(1, LANES), full(1, HD),
            full(SCONV_W, BW), full(HEADS, HD, HD), full(1, BW),
        ],
        out_specs=[
            pl.BlockSpec((N_BRANCH, tt, BW), lambda b, t: (0, b * nt + t, 0)),
            state(HEADS, HD, HD), state(HEADS, HD, HD),
            state(GDN_CONV - 1, 3 * BW), state(SCONV_W - 1, BW), state(POOL_HIST, BW),
        ],
        out_shape=[
            jax.ShapeDtypeStruct((N_BRANCH, nb * nt * tt, BW), BF16),
            jax.ShapeDtypeStruct((nb, HEADS, HD, HD), F32),
            jax.ShapeDtypeStruct((nb, HEADS, HD, HD), F32),
            jax.ShapeDtypeStruct((nb, GDN_CONV - 1, 3 * BW), F32),
            jax.ShapeDtypeStruct((nb, SCONV_W - 1, BW), F32),
            jax.ShapeDtypeStruct((nb, POOL_HIST, BW), F32),
        ],
        scratch_shapes=[
            pltpu.VMEM((HEADS, HD, HD), F32),
            pltpu.VMEM((HEADS, HD, HD), F32),
            pltpu.VMEM((tt + 8, 3 * BW), F32),
            pltpu.VMEM((tt + 8, BW), F32),
            pltpu.VMEM((tt + 16, BW), F32),
            pltpu.VMEM((HEADS, tt, tt), F32),
        ],
        compiler_params=pltpu.CompilerParams(
            dimension_semantics=("arbitrary", "arbitrary"), vmem_limit_bytes=VMEM_LIMIT),
        name="mix_prompt",
    )(z, zab, cos_t, sin_t, gdn_conv_w, alog_v, dtb_v, gdn_norm_g, sconv_w, pool_w16, pool_scale)


SB = 8


def _mix_sample_kernel(z_ref, zab_ref, cos_ref, sin_ref, convw_ref, alog_ref, dtb_ref, gng_ref,
                       scw_ref, poolw_ref, pscale_ref,
                       sret_in, sgdn_in, gconv_in, sconv_in, pool_in, *rest):
    ys_ref, sret_out, sgdn_out, gconv_out, sconv_out, pool_out, o_scr = rest[-7:]
    cos = cos_ref[...]
    sin = sin_ref[...]
    zero_rows = jnp.zeros((HD - SB, HD), F32)

    def columns(x):
        return jnp.concatenate([x, zero_rows], axis=0).T

    for h in range(HEADS):
        gamma = math.exp(_log_gamma(h))
        q = _rope(z_ref[:, C_RET_Q + h * HD:C_RET_Q + (h + 1) * HD], cos, sin)
        k = _rope(z_ref[:, C_RET_K + h * HD:C_RET_K + (h + 1) * HD], cos, sin) * (HD ** -0.5)
        v = z_ref[:, C_RET_V + h * HD:C_RET_V + (h + 1) * HD]
        q_t, k_t = columns(q), columns(k)
        for j in range(SB):
            s_new = sret_in[j, h] * gamma + k_t[:, j:j + 1] * v[j:j + 1, :]
            sret_out[j, h] = s_new
            o_scr[j:j + 1, h * HD:(h + 1) * HD] = jnp.sum(q_t[:, j:j + 1] * s_new, axis=0,
                                                          keepdims=True)

    u_qkv = z_ref[:, C_GDN_QKV:C_GDN_QKV + 3 * BW]
    conv = u_qkv * convw_ref[GDN_CONV - 1:GDN_CONV, :]
    for kk in range(GDN_CONV - 1):
        conv = conv + gconv_in[:, kk, :] * convw_ref[kk:kk + 1, :]
    qkv = _silu(conv)
    for kk in range(GDN_CONV - 2):
        gconv_out[:, kk, :] = gconv_in[:, kk + 1, :]
    gconv_out[:, GDN_CONV - 2, :] = u_qkv

    zab = zab_ref[...]
    decay_all = jnp.exp(-jnp.exp(alog_ref[...]) * _softplus(zab + dtb_ref[...]))
    beta_all = _sigmoid(zab)
    for h in range(HEADS):
        qh = qkv[:, h * HD:(h + 1) * HD]
        kh = qkv[:, BW + h * HD:BW + (h + 1) * HD]
        vh = qkv[:, 2 * BW + h * HD:2 * BW + (h + 1) * HD]
        qh = qh * lax.rsqrt(jnp.sum(qh * qh, axis=-1, keepdims=True) + EPS) * (HD ** -0.5)
        kh = kh * lax.rsqrt(jnp.sum(kh * kh, axis=-1, keepdims=True) + EPS)
        q_t, k_t = columns(qh), columns(kh)
        for j in range(SB):
            a = decay_all[j:j + 1, h:h + 1]
            beta = beta_all[j:j + 1, HEADS + h:HEADS + h + 1]
            s_dec = sgdn_in[j, h] * a
            k_col = k_t[:, j:j + 1]
            v_new = (vh[j:j + 1, :] - jnp.sum(k_col * s_dec, axis=0, keepdims=True)) * beta
            s_new = s_dec + k_col * v_new
            sgdn_out[j, h] = s_new
            o_scr[j:j + 1, BW + h * HD:BW + (h + 1) * HD] = jnp.sum(q_t[:, j:j + 1] * s_new,
                                                                    axis=0, keepdims=True)

    for h in range(HEADS):
        hs = slice(h * HD, (h + 1) * HD)
        ys_ref[0, :, hs] = (_rms_rows(o_scr[:, hs])
                            * _silu(z_ref[:, C_RET_G + h * HD:C_RET_G + (h + 1) * HD]))
        ys_ref[1, :, hs] = (_rms_rows(o_scr[:, BW + h * HD:BW + (h + 1) * HD]) * gng_ref[...]
                            * _silu(z_ref[:, C_GDN_G + h * HD:C_GDN_G + (h + 1) * HD]))

    prod = z_ref[:, C_SC_C:C_SC_C + BW] * z_ref[:, C_SC_H:C_SC_H + BW]
    conv = prod * scw_ref[SCONV_W - 1:SCONV_W, :]
    for kk in range(SCONV_W - 1):
        conv = conv + sconv_in[:, kk, :] * scw_ref[kk:kk + 1, :]
    ys_ref[2] = z_ref[:, C_SC_B:C_SC_B + BW] * conv * _silu(z_ref[:, C_SC_G:C_SC_G + BW])
    for kk in range(SCONV_W - 2):
        sconv_out[:, kk, :] = sconv_in[:, kk + 1, :]
    sconv_out[:, SCONV_W - 2, :] = prod

    pool_u = z_ref[:, C_POOL_U:C_POOL_U + BW]
    for gi, wnd in enumerate(POOL_WINDOWS):
        gs = slice(gi * HD, (gi + 1) * HD)
        cur = pool_u[:, gs]
        acc = cur
        for d in range(1, wnd):
            acc = acc + pool_in[:, POOL_HIST - d, gs]
        pooled = acc / float(min(PAST_LEN + 1, wnd)) - cur
        mixed = _dot(pooled.astype(BF16), poolw_ref[gi])
        ys_ref[3, :, gs] = (mixed * pscale_ref[:, gs]
                            * _silu(z_ref[:, C_POOL_G + gi * HD:C_POOL_G + (gi + 1) * HD]))
    for r in range(POOL_HIST - 1):
        pool_out[:, r, :] = pool_in[:, r + 1, :]
    pool_out[:, POOL_HIST - 1, :] = pool_u


def _mix_sample(z, zab, cos_t, sin_t, gdn_conv_w, alog_v, dtb_v, gdn_norm_g, sconv_w, pool_w16,
                pool_scale, state_ret, state_gdn, state_gdn_conv, state_sconv, state_pool, layer,
                big_states):
    ns = z.shape[0]
    depth = state_ret.shape[0]
    full = lambda *shape: pl.BlockSpec((None,) + shape, lambda i: (layer,) + (0,) * len(shape))
    st_in = lambda *shape: pl.BlockSpec((None, SB) + shape, lambda i: (layer, i) + (0,) * len(shape))
    st_out = lambda *shape: pl.BlockSpec((SB,) + shape, lambda i: (i,) + (0,) * len(shape))
    big_shape = jax.ShapeDtypeStruct((depth, ns, HEADS, HD, HD), F32)
    n_fixed = 16
    aliased = [] if big_states is None else list(big_states)
    return pl.pallas_call(
        _mix_sample_kernel,
        grid=(ns // SB,),
        in_specs=[
            pl.BlockSpec((SB, N_MIX), lambda i: (i, 0)),
            pl.BlockSpec((SB, LANES), lambda i: (i, 0)),
            pl.BlockSpec((1, HD), lambda i: (0, 0)),
            pl.BlockSpec((1, HD), lambda i: (0, 0)),
            full(GDN_CONV, 3 * BW), full(1, LANES), full(1, LANES), full(1, HD),
            full(SCONV_W, BW), full(HEADS, HD, HD), full(1, BW),
            st_in(HEADS, HD, HD), st_in(HEADS, HD, HD),
            st_in(GDN_CONV - 1, 3 * BW), st_in(SCONV_W - 1, BW), st_in(POOL_HIST, BW),
        ] + [pl.BlockSpec(memory_space=pl.ANY)] * len(aliased),
        out_specs=[
            pl.BlockSpec((N_BRANCH, SB, BW), lambda i: (0, i, 0)),
            st_in(HEADS, HD, HD), st_in(HEADS, HD, HD),
            st_out(GDN_CONV - 1, 3 * BW), st_out(SCONV_W - 1, BW), st_out(POOL_HIST, BW),
        ],
        out_shape=[
            jax.ShapeDtypeStruct((N_BRANCH, ns, BW), F32),
            big_shape, big_shape,
            jax.ShapeDtypeStruct((ns, GDN_CONV - 1, 3 * BW), F32),
            jax.ShapeDtypeStruct((ns, SCONV_W - 1, BW), F32),
            jax.ShapeDtypeStruct((ns, POOL_HIST, BW), F32),
        ],
        input_output_aliases={n_fixed + i: 1 + i for i in range(len(aliased))},
        scratch_shapes=[pltpu.VMEM((SB, 2 * BW), F32)],
        compiler_params=pltpu.CompilerParams(
            dimension_semantics=("arbitrary",), vmem_limit_bytes=VMEM_LIMIT),
        name="mix_sample",
    )(z, zab, cos_t, sin_t, gdn_conv_w, alog_v, dtb_v, gdn_norm_g, sconv_w, pool_w16, pool_scale,
      state_ret, state_gdn, state_gdn_conv, state_sconv, state_pool, *aliased)


N_CT = D_MODEL // WCOL
N_GATE_PHASES = N_CT * N_BRANCH
N_OUT_PHASES = N_GATE_PHASES + N_CT


def _out_kernel(x_ref, g_ref, ys_ref, wg_ref, wb_ref, wo_ref, xc_ref, o_ref, h_ref, acc_ref,
                merged_ref, *, row_splits):
    p = pl.program_id(1)
    n = p % N_BRANCH
    tm = x_ref.shape[0]
    rows = tm // row_splits

    @pl.when(p == 0)
    def _():
        h_ref[...] = (_rms_rows(x_ref[...]) * g_ref[...]).astype(BF16)
        acc_ref[...] = jnp.zeros_like(acc_ref)

    @pl.when(p < N_GATE_PHASES)
    def _():
        c = p // N_BRANCH
        for r in range(row_splits):
            rs = slice(r * rows, (r + 1) * rows)
            gate = _sigmoid(_dot(h_ref[rs, :], wg_ref[...]))
            contrib = gate * _dot(ys_ref[rs, :].astype(BF16), wb_ref[...])
            acc = jnp.where(n == 0, contrib, acc_ref[rs, :] + contrib)
            acc_ref[rs, :] = acc
            merged_ref[c, rs, :] = acc.astype(BF16)

    @pl.when(p >= N_GATE_PHASES)
    def _():
        acc = xc_ref[...]
        for c in range(N_CT):
            acc = acc + _dot(merged_ref[c], wo_ref[c * WCOL:(c + 1) * WCOL, :])
        o_ref[...] = acc


def _out_proj(x, norm_g, ys, w_g, w_b, w_o, layer, tm):
    m = x.shape[0]
    gate_phase = lambda p: jnp.minimum(p, N_GATE_PHASES - 1)
    branch = lambda p: gate_phase(p) % N_BRANCH
    col_tile = lambda p: gate_phase(p) // N_BRANCH
    out_tile = lambda p: jnp.maximum(p - N_GATE_PHASES, 0)
    return pl.pallas_call(
        functools.partial(_out_kernel, row_splits=2 if tm >= 256 else 1),
        grid=(m // tm, N_OUT_PHASES),
        in_specs=[
            pl.BlockSpec((tm, D_MODEL), lambda i, p: (i, 0)),
            pl.BlockSpec((None, 1, D_MODEL), lambda i, p: (layer, 0, 0)),
            pl.BlockSpec((None, tm, BW), lambda i, p: (branch(p), i, 0)),
            pl.BlockSpec((None, D_MODEL, WCOL),
                         lambda i, p: (layer, 0, branch(p) * N_CT + col_tile(p))),
            pl.BlockSpec((None, None, BW, WCOL), lambda i, p: (layer, branch(p), 0, col_tile(p))),
            pl.BlockSpec((None, D_MODEL, WCOL), lambda i, p: (layer, 0, out_tile(p))),
            pl.BlockSpec((tm, WCOL), lambda i, p: (i, out_tile(p))),
        ],
        out_specs=pl.BlockSpec((tm, WCOL), lambda i, p: (i, out_tile(p))),
        out_shape=jax.ShapeDtypeStruct((m, D_MODEL), F32),
        scratch_shapes=[pltpu.VMEM((tm, D_MODEL), BF16), pltpu.VMEM((tm, WCOL), F32),
                        pltpu.VMEM((N_CT, tm, WCOL), BF16)],
        compiler_params=pltpu.CompilerParams(
            dimension_semantics=("arbitrary", "arbitrary"), vmem_limit_bytes=VMEM_LIMIT),
        name="out_proj",
    )(x, norm_g, ys, w_g, w_b, w_o, x)


def _final_norm_kernel(x_ref, g_ref, o_ref):
    o_ref[...] = _rms_rows(x_ref[...]) * g_ref[...]


def _final_norm(x, g, tm):
    m = x.shape[0]
    return pl.pallas_call(
        _final_norm_kernel,
        grid=(m // tm,),
        in_specs=[pl.BlockSpec((tm, D_MODEL), lambda i: (i, 0)),
                  pl.BlockSpec((1, D_MODEL), lambda i: (0, 0))],
        out_specs=pl.BlockSpec((tm, D_MODEL), lambda i: (i, 0)),
        out_shape=jax.ShapeDtypeStruct((m, D_MODEL), F32),
        compiler_params=pltpu.CompilerParams(dimension_semantics=("arbitrary",)),
        name="final_norm",
    )(x, g)


def _rope_tables(pos):
    half = HD // 2
    inv = ROPE_BASE ** (-jnp.arange(half, dtype=F32) / half)
    ang = pos.astype(F32)[:, None] * inv[None, :]
    cos, sin = jnp.cos(ang), jnp.sin(ang)
    return jnp.concatenate([cos, cos], axis=1), jnp.concatenate([-sin, sin], axis=1)


def _prepare_weights(w_in, w_out):
    ab0 = 8 * BW
    g0 = N_MIX + 2 * HEADS
    w_za = w_in[:, :, :ab0].astype(BF16)
    w_zb = w_in[:, :, ab0 + 2 * HEADS:g0].astype(BF16)
    w_ab = jnp.pad(w_in[:, :, ab0:ab0 + 2 * HEADS],
                   ((0, 0), (0, 0), (0, LANES - 2 * HEADS))).astype(BF16)
    w_g = w_in[:, :, g0:].astype(BF16)
    return w_za, w_zb, w_ab, w_g, w_out.astype(BF16)


def kernel(x_prompt, x_sample, state_ret, state_gdn, state_gdn_conv, state_sconv, state_pool,
           norm_g, w_in, gdn_conv_w, gdn_a_log, gdn_dt_bias, gdn_norm_g, sconv_w, pool_w, pool_scale,
           w_branch, w_out, final_norm_g):
    bp, tp, _ = x_prompt.shape
    ns, ts, _ = x_sample.shape
    depth = w_in.shape[0]
    tt = min(256, tp)
    nt = tp // tt
    tm_in = min(1024, bp * tp)
    tm_out = min(512, bp * tp)

    w_za, w_zb, w_ab, w_g, w_o = _prepare_weights(w_in, w_out)
    w_b = w_branch.astype(BF16)
    pool_w16 = pool_w.astype(BF16)
    pad_h = lambda a: jnp.pad(a, ((0, 0), (0, LANES - HEADS)))[:, None, :]
    alog_v, dtb_v = pad_h(gdn_a_log), pad_h(gdn_dt_bias)
    norm_g3 = norm_g[:, None, :]
    gng3 = gdn_norm_g[:, None, :]
    pscale3 = pool_scale[:, None, :]

    cos_p, sin_p = _rope_tables(jnp.arange(tp, dtype=jnp.int32))
    cos_s, sin_s = _rope_tables(PAST_LEN + jnp.arange(ts, dtype=jnp.int32))

    hp = x_prompt.reshape(bp * tp, D_MODEL)
    hs = x_sample.reshape(ns * ts, D_MODEL)
    new_p = [[] for _ in range(5)]
    new_s = [[] for _ in range(3)]
    big_states = None
    for l in range(depth):
        z, zab = _in_proj(hp, norm_g3, w_za, w_zb, w_ab, l, tm=tm_in)
        ys, *st_p = _mix_prompt(z, zab, cos_p, sin_p, gdn_conv_w, alog_v, dtb_v, gng3, sconv_w,
                                pool_w16, pscale3, l, bp, nt, tt)
        hp = _out_proj(hp, norm_g3, ys, w_g, w_b, w_o, l, tm=tm_out)

        z, zab = _in_proj(hs, norm_g3, w_za, w_zb, w_ab, l, tm=ns)
        ys, s_ret, s_gdn, *st_s = _mix_sample(
            z, zab, cos_s, sin_s, gdn_conv_w, alog_v, dtb_v, gng3, sconv_w, pool_w16, pscale3,
            state_ret, state_gdn, state_gdn_conv, state_sconv, state_pool, l, big_states)
        big_states = (s_ret, s_gdn)
        hs = _out_proj(hs, norm_g3, ys, w_g, w_b, w_o, l, tm=ns)
        for i in range(5):
            new_p[i].append(st_p[i])
        for i in range(3):
            new_s[i].append(st_s[i])

    fg = final_norm_g[None, :]
    y_prompt = _final_norm(hp, fg, 512).reshape(bp, tp, D_MODEL)
    y_sample = _final_norm(hs, fg, ns).reshape(ns, ts, D_MODEL)
    outs_p = [jnp.stack(a, axis=0) for a in new_p]
    outs_s = [jnp.stack(a, axis=0) for a in new_s]
    return (y_prompt, y_sample, *outs_p, *big_states, *outs_s)
```

```python
import functools
import math

import jax
import jax.numpy as jnp
from jax import lax
from jax.experimental import pallas as pl
from jax.experimental.pallas import tpu as pltpu

F32 = jnp.float32
BF16 = jnp.bfloat16

D_MODEL = 2048
N_BRANCH = 4
BW = D_MODEL // N_BRANCH
HEADS = 4
HD = BW // HEADS
ROPE_BASE = 10000.0
GDN_CONV = 4
SCONV_W = 3
POOL_WINDOWS = (2, 4, 8, 16)
POOL_HIST = 15
GCHUNK = 64
EPS = 1e-6
PAST_LEN = 16384
N_MIX = 14 * BW
N_GATE = N_BRANCH * D_MODEL
LANES = 128
WCOL = 1024
VMEM_LIMIT = 56 * 1024 * 1024

C_RET_Q, C_RET_K, C_RET_V, C_RET_G = 0, BW, 2 * BW, 3 * BW
C_GDN_QKV, C_GDN_G = 4 * BW, 7 * BW
C_SC_B, C_SC_C, C_SC_H, C_SC_G = 8 * BW, 9 * BW, 10 * BW, 11 * BW
C_POOL_U, C_POOL_G = 12 * BW, 13 * BW


def _dot(a, b):
    return jnp.dot(a, b, preferred_element_type=F32)


def _dot_nt(a, b):
    return lax.dot_general(a, b, (((1,), (1,)), ((), ())), preferred_element_type=F32)


def _dot_tn(a, b):
    return lax.dot_general(a, b, (((0,), (0,)), ((), ())), preferred_element_type=F32)


def _split(x):
    hi = x.astype(BF16)
    return hi, (x - hi.astype(F32)).astype(BF16)


def _sigmoid(x):
    return 0.5 * jnp.tanh(0.5 * x) + 0.5


def _silu(x):
    return x * _sigmoid(x)


def _softplus(x):
    return jnp.maximum(x, 0.0) + jnp.log1p(jnp.exp(-jnp.abs(x)))


def _rms_rows(x):
    return x * lax.rsqrt(jnp.mean(x * x, axis=-1, keepdims=True) + EPS)


def _rope(x, cos, sin_signed):
    return x * cos + pltpu.roll(x, HD // 2, axis=1) * sin_signed


def _log_gamma(h):
    return math.log(1.0 - 2.0 ** (-5.0 - h))


N_A_BLOCKS = (8 * BW) // WCOL
N_MIX_BLOCKS = N_MIX // WCOL


def _in_proj_kernel(x_ref, g_ref, wa_ref, wb_ref, wab_ref, z_ref, zab_ref, h_ref):
    j = pl.program_id(1)

    @pl.when(j == 0)
    def _():
        hb = (_rms_rows(x_ref[...]) * g_ref[...]).astype(BF16)
        h_ref[...] = hb
        zab_ref[...] = _dot(hb, wab_ref[...])

    @pl.when(j < N_A_BLOCKS)
    def _():
        z_ref[...] = _dot(h_ref[...], wa_ref[...])

    @pl.when(j >= N_A_BLOCKS)
    def _():
        z_ref[...] = _dot(h_ref[...], wb_ref[...])


def _in_proj(x, norm_g, w_a, w_b, w_ab, layer, tm):
    m = x.shape[0]
    return pl.pallas_call(
        _in_proj_kernel,
        grid=(m // tm, N_MIX_BLOCKS),
        in_specs=[
            pl.BlockSpec((tm, D_MODEL), lambda i, j: (i, 0)),
            pl.BlockSpec((None, 1, D_MODEL), lambda i, j: (layer, 0, 0)),
            pl.BlockSpec((None, D_MODEL, WCOL),
                         lambda i, j: (layer, 0, jnp.minimum(j, N_A_BLOCKS - 1))),
            pl.BlockSpec((None, D_MODEL, WCOL),
                         lambda i, j: (layer, 0, jnp.maximum(j - N_A_BLOCKS, 0))),
            pl.BlockSpec((None, D_MODEL, LANES), lambda i, j: (layer, 0, 0)),
        ],
        out_specs=[
            pl.BlockSpec((tm, WCOL), lambda i, j: (i, j)),
            pl.BlockSpec((tm, LANES), lambda i, j: (i, 0)),
        ],
        out_shape=[
            jax.ShapeDtypeStruct((m, N_MIX), F32),
            jax.ShapeDtypeStruct((m, LANES), F32),
        ],
        scratch_shapes=[pltpu.VMEM((tm, D_MODEL), BF16)],
        compiler_params=pltpu.CompilerParams(
            dimension_semantics=("arbitrary", "arbitrary"), vmem_limit_bytes=VMEM_LIMIT),
        name="in_proj",
    )(x, norm_g, w_a, w_b, w_ab)


def _packed_products(lhs_f32, m_hi_bd, m_lo_bd):
    parts = [_split(x) for x in lhs_f32]
    stack_all = jnp.concatenate([p for hl in parts for p in hl], axis=0)
    stack_hi = jnp.concatenate([hl[0] for hl in parts], axis=0)
    r1 = _dot(stack_all, m_hi_bd)
    r2 = _dot(stack_hi, m_lo_bd)
    out = []
    for i in range(len(lhs_f32)):
        a = 2 * i * GCHUNK
        b = i * GCHUNK
        out.append(r1[a:a + GCHUNK] + r1[a + GCHUNK:a + 2 * GCHUNK] + r2[b:b + GCHUNK])
    return out


def _block_diag(x16, bd_mask16):
    return jnp.concatenate([x16] * HEADS, axis=0) * bd_mask16


def _neumann_inverse_packed(l_ps, eye_p, bd_mask16):
    bd = lambda m: tuple(_block_diag(x, bd_mask16) for x in _split(m))
    ps = [eye_p - l_p for l_p in l_ps]
    ms = [_packed_products([l_p], *bd(l_p))[0] for l_p in l_ps]
    steps = int(math.log2(GCHUNK)) - 1
    for s in range(steps):
        for c in range(len(l_ps)):
            if s + 1 < steps:
                pm, ms[c] = _packed_products([ps[c], ms[c]], *bd(ms[c]))
            else:
                (pm,) = _packed_products([ps[c]], *bd(ms[c]))
            ps[c] = ps[c] + pm
    return ps


def _mix_prompt_kernel(z_ref, zab_ref, cos_ref, sin_ref, convw_ref, alog_ref, dtb_ref, gng_ref,
                       scw_ref, poolw_ref, pscale_ref,
                       ys_ref, pret_ref, pgdn_ref, pgconv_ref, psconv_ref, ppool_ref,
                       sret, sgdn, extq, extsc, extpool, dmask_ref, *, tt):
    b = pl.program_id(0)
    t = pl.program_id(1)
    n_t = pl.num_programs(1)
    n_chunk = tt // GCHUNK

    @pl.when((b == 0) & (t == 0))
    def _():
        ii = lax.broadcasted_iota(jnp.int32, (tt, tt), 0)
        jj = lax.broadcasted_iota(jnp.int32, (tt, tt), 1)
        causal = ii >= jj
        dif = jnp.where(causal, ii - jj, 0).astype(F32)
        for h in range(HEADS):
            dmask_ref[h] = jnp.where(causal, jnp.exp(dif * _log_gamma(h)), 0.0)

    @pl.when(t == 0)
    def _():
        sret[...] = jnp.zeros_like(sret)
        sgdn[...] = jnp.zeros_like(sgdn)
        extq[0:8, :] = jnp.zeros((8, 3 * BW), F32)
        extsc[0:8, :] = jnp.zeros((8, BW), F32)
        extpool[0:16, :] = jnp.zeros((16, BW), F32)

    cos = cos_ref[...]
    sin = sin_ref[...]

    ipos = lax.broadcasted_iota(jnp.int32, (tt, 1), 0).astype(F32)

    def retention_head(h):
        lg = _log_gamma(h)
        hs = slice(h * HD, (h + 1) * HD)
        q_scale = jnp.exp((ipos + 1.0) * lg)
        k_scale = jnp.exp((tt - 1.0 - ipos) * lg)
        q = _rope(z_ref[:, C_RET_Q + h * HD:C_RET_Q + (h + 1) * HD], cos, sin)
        k = _rope(z_ref[:, C_RET_K + h * HD:C_RET_K + (h + 1) * HD], cos, sin) * (HD ** -0.5)
        vb = z_ref[:, C_RET_V + h * HD:C_RET_V + (h + 1) * HD].astype(BF16)
        scores = _dot_nt(q.astype(BF16), k.astype(BF16)) * dmask_ref[h]
        s0 = sret[h]
        o = _dot(scores.astype(BF16), vb) + _dot((q * q_scale).astype(BF16), s0.astype(BF16))
        kv = _dot_tn((k * k_scale).astype(BF16), vb)
        sret[h] = s0 * math.exp(tt * lg) + kv
        ys_ref[0, :, hs] = (_rms_rows(o) * _silu(z_ref[:, C_RET_G + h * HD:C_RET_G + (h + 1) * HD])
                            ).astype(ys_ref.dtype)

    extq[8:8 + tt, :] = z_ref[:, C_GDN_QKV:C_GDN_QKV + 3 * BW]
    conv = extq[5:5 + tt, :] * convw_ref[0:1, :]
    for kk in range(1, GDN_CONV):
        conv = conv + extq[5 + kk:5 + kk + tt, :] * convw_ref[kk:kk + 1, :]
    qkv = _silu(conv)
    extq[0:8, :] = extq[tt:tt + 8, :]

    zab = zab_ref[...]
    g_all = -jnp.exp(alog_ref[...]) * _softplus(zab + dtb_ref[...])
    beta_all = _sigmoid(zab)
    rin = lax.broadcasted_iota(jnp.int32, (tt, LANES), 0) & (GCHUNK - 1)
    b_all = g_all
    sh = 1
    while sh < GCHUNK:
        b_all = b_all + jnp.where(rin >= sh, pltpu.roll(b_all, sh, axis=0), 0.0)
        sh *= 2
    b_all_t = b_all.T

    ci = lax.broadcasted_iota(jnp.int32, (GCHUNK, GCHUNK), 0)
    cj = lax.broadcasted_iota(jnp.int32, (GCHUNK, GCHUNK), 1)
    c_causal = ci >= cj
    c_strict = ci > cj
    pi = lax.broadcasted_iota(jnp.int32, (GCHUNK, HEADS * GCHUNK), 0)
    pj = lax.broadcasted_iota(jnp.int32, (GCHUNK, HEADS * GCHUNK), 1)
    eye_p = jnp.where(pi == (pj & (GCHUNK - 1)), 1.0, 0.0).astype(F32)
    bi = lax.broadcasted_iota(jnp.int32, (HEADS * GCHUNK, HEADS * GCHUNK), 0)
    bj = lax.broadcasted_iota(jnp.int32, (HEADS * GCHUNK, HEADS * GCHUNK), 1)
    bd_mask16 = jnp.where((bi ^ bj) < GCHUNK, 1.0, 0.0).astype(BF16)

    qs, ks, vs = [], [], []
    for h in range(HEADS):
        qh = qkv[:, h * HD:(h + 1) * HD]
        kh = qkv[:, BW + h * HD:BW + (h + 1) * HD]
        qs.append(qh * lax.rsqrt(jnp.sum(qh * qh, axis=-1, keepdims=True) + EPS) * (HD ** -0.5))
        ks.append(kh * lax.rsqrt(jnp.sum(kh * kh, axis=-1, keepdims=True) + EPS))
        vs.append(qkv[:, 2 * BW + h * HD:2 * BW + (h + 1) * HD])

    prep, l_ps = [], []
    for c in range(n_chunk):
        rs = slice(c * GCHUNK, (c + 1) * GCHUNK)
        per_head, l_blocks = [], []
        for h in range(HEADS):
            q, k, v = qs[h][rs], ks[h][rs], vs[h][rs]
            bcol = b_all[rs, h:h + 1]
            brow = b_all_t[h:h + 1, rs]
            beta = beta_all[rs, HEADS + h:HEADS + h + 1]
            blast = b_all[c * GCHUNK + GCHUNK - 1:(c + 1) * GCHUNK, h:h + 1]
            eb = jnp.exp(bcol)
            decay = jnp.where(c_causal, jnp.exp(jnp.where(c_causal, bcol - brow, 0.0)), 0.0)
            kb = k * beta
            k16 = k.astype(BF16)
            kq = _dot_nt(jnp.concatenate([kb.astype(BF16), q.astype(BF16)], axis=0), k16)
            l_blocks.append(jnp.where(c_strict, kq[:GCHUNK] * decay, 0.0))
            attn16 = (kq[GCHUNK:] * decay).astype(BF16)
            rhs = jnp.concatenate([v * beta, kb * eb], axis=1)
            qe16 = (q * eb).astype(BF16)
            kdec16 = (k * jnp.exp(blast - bcol)).astype(BF16)
            per_head.append((rhs, attn16, qe16, kdec16, jnp.exp(blast)))
        prep.append(per_head)
        l_ps.append(jnp.concatenate(l_blocks, axis=1))
    t_ps = _neumann_inverse_packed(l_ps, eye_p, bd_mask16)
    sols = []
    for c in range(n_chunk):
        sols_c = []
        for h in range(HEADS):
            t_hi, t_lo = _split(t_ps[c][:, h * GCHUNK:(h + 1) * GCHUNK])
            r_hi, r_lo = _split(prep[c][h][0])
            r1 = _dot(jnp.concatenate([t_hi, t_lo], axis=0), r_hi)
            sols_c.append(r1[:GCHUNK] + r1[GCHUNK:] + _dot(t_hi, r_lo))
        sols.append(sols_c)

    states = [sgdn[h] for h in range(HEADS)]
    o_chunks = [[] for _ in range(HEADS)]
    ret_heads = list(range(HEADS))
    for c in range(n_chunk):
        for h in range(HEADS):
            _, attn16, qe16, kdec16, e_last = prep[c][h]
            u, w = sols[c][h][:, :HD], sols[c][h][:, HD:]
            ws_qs = _dot(jnp.concatenate([w.astype(BF16), qe16], axis=0), states[h].astype(BF16))
            v_new16 = (u - ws_qs[:GCHUNK]).astype(BF16)
            o_chunks[h].append(ws_qs[GCHUNK:] + _dot(attn16, v_new16))
            states[h] = states[h] * e_last + _dot_tn(kdec16, v_new16)
        for h in ret_heads[c::n_chunk]:
            retention_head(h)
    for h in range(HEADS):
        sgdn[h] = states[h]
        o = jnp.concatenate(o_chunks[h], axis=0)
        ys_ref[1, :, h * HD:(h + 1) * HD] = (
            _rms_rows(o) * gng_ref[...] * _silu(z_ref[:, C_GDN_G + h * HD:C_GDN_G + (h + 1) * HD])
        ).astype(ys_ref.dtype)

    extsc[8:8 + tt, :] = z_ref[:, C_SC_C:C_SC_C + BW] * z_ref[:, C_SC_H:C_SC_H + BW]
    conv = extsc[6:6 + tt, :] * scw_ref[0:1, :]
    for kk in range(1, SCONV_W):
        conv = conv + extsc[6 + kk:6 + kk + tt, :] * scw_ref[kk:kk + 1, :]
    ys_ref[2] = (z_ref[:, C_SC_B:C_SC_B + BW] * conv * _silu(z_ref[:, C_SC_G:C_SC_G + BW])
                 ).astype(ys_ref.dtype)
    extsc[0:8, :] = extsc[tt:tt + 8, :]

    extpool[16:16 + tt, :] = z_ref[:, C_POOL_U:C_POOL_U + BW]
    pos1 = (t * tt + lax.broadcasted_iota(jnp.int32, (tt, 1), 0) + 1).astype(F32)
    for gi, wnd in enumerate(POOL_WINDOWS):
        gs = slice(gi * HD, (gi + 1) * HD)
        acc = extpool[:, gs]
        span = 1
        while span < wnd:
            acc = acc + pltpu.roll(acc, span, axis=0)
            span *= 2
        cur = z_ref[:, C_POOL_U + gi * HD:C_POOL_U + (gi + 1) * HD]
        pooled = acc[16:] / jnp.minimum(pos1, float(wnd)) - cur
        mixed = _dot(pooled.astype(BF16), poolw_ref[gi])
        ys_ref[3, :, gs] = (mixed * pscale_ref[:, gs]
                            * _silu(z_ref[:, C_POOL_G + gi * HD:C_POOL_G + (gi + 1) * HD])
                            ).astype(ys_ref.dtype)
    extpool[0:16, :] = extpool[tt:tt + 16, :]

    @pl.when(t == n_t - 1)
    def _():
        pret_ref[...] = sret[...]
        pgdn_ref[...] = sgdn[...]
        pgconv_ref[...] = extq[8 - (GDN_CONV - 1):8, :]
        psconv_ref[...] = extsc[8 - (SCONV_W - 1):8, :]
        ppool_ref[...] = extpool[16 - POOL_HIST:16, :]


def _mix_prompt(z, zab, cos_t, sin_t, gdn_conv_w, alog_v, dtb_v, gdn_norm_g, sconv_w, pool_w16,
                pool_scale, layer, nb, nt, tt):
    full = lambda *shape: pl.BlockSpec((None,) + shape, lambda b, t: (layer,) + (0,) * len(shape))
    state = lambda *shape: pl.BlockSpec((None,) + shape, lambda b, t: (b,) + (0,) * len(shape))
    return pl.pallas_call(
        functools.partial(_mix_prompt_kernel, tt=tt),
        grid=(nb, nt),
        in_specs=[
            pl.BlockSpec((tt, N_MIX), lambda b, t: (b * nt + t, 0)),
            pl.BlockSpec((tt, LANES), lambda b, t: (b * nt + t, 0)),
            pl.BlockSpec((tt, HD), lambda b, t: (t, 0)),
            pl.BlockSpec((tt, HD), lambda b, t: (t, 0)),
            full(GDN_CONV, 3 * BW), full(1, LANES), full(1, LANES), full(1, HD),
            full(SCONV_W, BW), full(HEADS, HD, HD), full(1, BW),
        ],
        out_specs=[
            pl.BlockSpec((N_BRANCH, tt, BW), lambda b, t: (0, b * nt + t, 0)),
            state(HEADS, HD, HD), state(HEADS, HD, HD),
            state(GDN_CONV - 1, 3 * BW), state(SCONV_W - 1, BW), state(POOL_HIST, BW),
        ],
        out_shape=[
            jax.ShapeDtypeStruct((N_BRANCH, nb * nt * tt, BW), BF16),
            jax.ShapeDtypeStruct((nb, HEADS, HD, HD), F32),
            jax.ShapeDtypeStruct((nb, HEADS, HD, HD), F32),
            jax.ShapeDtypeStruct((nb, GDN_CONV - 1, 3 * BW), F32),
            jax.ShapeDtypeStruct((nb, SCONV_W - 1, BW), F32),
            jax.ShapeDtypeStruct((nb, POOL_HIST, BW), F32),
        ],
        scratch_shapes=[
            pltpu.VMEM((HEADS, HD, HD), F32),
            pltpu.VMEM((HEADS, HD, HD), F32),
            pltpu.VMEM((tt + 8, 3 * BW), F32),
            pltpu.VMEM((tt + 8, BW), F32),
            pltpu.VMEM((tt + 16, BW), F32),
            pltpu.VMEM((HEADS, tt, tt), F32),
        ],
        compiler_params=pltpu.CompilerParams(
            dimension_semantics=("arbitrary", "arbitrary"), vmem_limit_bytes=VMEM_LIMIT),
        name="mix_prompt",
    )(z, zab, cos_t, sin_t, gdn_conv_w, alog_v, dtb_v, gdn_norm_g, sconv_w, pool_w16, pool_scale)


SB = 8


def _mix_sample_kernel(z_ref, zab_ref, cos_ref, sin_ref, convw_ref, alog_ref, dtb_ref, gng_ref,
                       scw_ref, poolw_ref, pscale_ref,
                       sret_in, sgdn_in, gconv_in, sconv_in, pool_in, *rest):
    ys_ref, sret_out, sgdn_out, gconv_out, sconv_out, pool_out, o_scr = rest[-7:]
    cos = cos_ref[...]
    sin = sin_ref[...]
    zero_rows = jnp.zeros((HD - SB, HD), F32)

    def columns(x):
        return jnp.concatenate([x, zero_rows], axis=0).T

    for h in range(HEADS):
        gamma = math.exp(_log_gamma(h))
        q = _rope(z_ref[:, C_RET_Q + h * HD:C_RET_Q + (h + 1) * HD], cos, sin)
        k = _rope(z_ref[:, C_RET_K + h * HD:C_RET_K + (h + 1) * HD], cos, sin) * (HD ** -0.5)
        v = z_ref[:, C_RET_V + h * HD:C_RET_V + (h + 1) * HD]
        q_t, k_t = columns(q), columns(k)
        for j in range(SB):
            s_new = sret_in[j, h] * gamma + k_t[:, j:j + 1] * v[j:j + 1, :]
            sret_out[j, h] = s_new
            o_scr[j:j + 1, h * HD:(h + 1) * HD] = jnp.sum(q_t[:, j:j + 1] * s_new, axis=0,
                                                          keepdims=True)

    u_qkv = z_ref[:, C_GDN_QKV:C_GDN_QKV + 3 * BW]
    conv = u_qkv * convw_ref[GDN_CONV - 1:GDN_CONV, :]
    for kk in range(GDN_CONV - 1):
        conv = conv + gconv_in[:, kk, :] * convw_ref[kk:kk + 1, :]
    qkv = _silu(conv)
    for kk in range(GDN_CONV - 2):
        gconv_out[:, kk, :] = gconv_in[:, kk + 1, :]
    gconv_out[:, GDN_CONV - 2, :] = u_qkv

    zab = zab_ref[...]
    decay_all = jnp.exp(-jnp.exp(alog_ref[...]) * _softplus(zab + dtb_ref[...]))
    beta_all = _sigmoid(zab)
    for h in range(HEADS):
        qh = qkv[:, h * HD:(h + 1) * HD]
        kh = qkv[:, BW + h * HD:BW + (h + 1) * HD]
        vh = qkv[:, 2 * BW + h * HD:2 * BW + (h + 1) * HD]
        qh = qh * lax.rsqrt(jnp.sum(qh * qh, axis=-1, keepdims=True) + EPS) * (HD ** -0.5)
        kh = kh * lax.rsqrt(jnp.sum(kh * kh, axis=-1, keepdims=True) + EPS)
        q_t, k_t = columns(qh), columns(kh)
        for j in range(SB):
            a = decay_all[j:j + 1, h:h + 1]
            beta = beta_all[j:j + 1, HEADS + h:HEADS + h + 1]
            s_dec = sgdn_in[j, h] * a
            k_col = k_t[:, j:j + 1]
            v_new = (vh[j:j + 1, :] - jnp.sum(k_col * s_dec, axis=0, keepdims=True)) * beta
            s_new = s_dec + k_col * v_new
            sgdn_out[j, h] = s_new
            o_scr[j:j + 1, BW + h * HD:BW + (h + 1) * HD] = jnp.sum(q_t[:, j:j + 1] * s_new,
                                                                    axis=0, keepdims=True)

    for h in range(HEADS):
        hs = slice(h * HD, (h + 1) * HD)
        ys_ref[0, :, hs] = (_rms_rows(o_scr[:, hs])
                            * _silu(z_ref[:, C_RET_G + h * HD:C_RET_G + (h + 1) * HD]))
        ys_ref[1, :, hs] = (_rms_rows(o_scr[:, BW + h * HD:BW + (h + 1) * HD]) * gng_ref[...]
                            * _silu(z_ref[:, C_GDN_G + h * HD:C_GDN_G + (h + 1) * HD]))

    prod = z_ref[:, C_SC_C:C_SC_C + BW] * z_ref[:, C_SC_H:C_SC_H + BW]
    conv = prod * scw_ref[SCONV_W - 1:SCONV_W, :]
    for kk in range(SCONV_W - 1):
        conv = conv + sconv_in[:, kk, :] * scw_ref[kk:kk + 1, :]
    ys_ref[2] = z_ref[:, C_SC_B:C_SC_B + BW] * conv * _silu(z_ref[:, C_SC_G:C_SC_G + BW])
    for kk in range(SCONV_W - 2):
        sconv_out[:, kk, :] = sconv_in[:, kk + 1, :]
    sconv_out[:, SCONV_W - 2, :] = prod

    pool_u = z_ref[:, C_POOL_U:C_POOL_U + BW]
    for gi, wnd in enumerate(POOL_WINDOWS):
        gs = slice(gi * HD, (gi + 1) * HD)
        cur = pool_u[:, gs]
        acc = cur
        for d in range(1, wnd):
            acc = acc + pool_in[:, POOL_HIST - d, gs]
        pooled = acc / float(min(PAST_LEN + 1, wnd)) - cur
        mixed = _dot(pooled.astype(BF16), poolw_ref[gi])
        ys_ref[3, :, gs] = (mixed * pscale_ref[:, gs]
                            * _silu(z_ref[:, C_POOL_G + gi * HD:C_POOL_G + (gi + 1) * HD]))
    for r in range(POOL_HIST - 1):
        pool_out[:, r, :] = pool_in[:, r + 1, :]
    pool_out[:, POOL_HIST - 1, :] = pool_u


def _mix_sample(z, zab, cos_t, sin_t, gdn_conv_w, alog_v, dtb_v, gdn_norm_g, sconv_w, pool_w16,
                pool_scale, state_ret, state_gdn, state_gdn_conv, state_sconv, state_pool, layer,
                big_states):
    ns = z.shape[0]
    depth = state_ret.shape[0]
    full = lambda *shape: pl.BlockSpec((None,) + shape, lambda i: (layer,) + (0,) * len(shape))
    st_in = lambda *shape: pl.BlockSpec((None, SB) + shape, lambda i: (layer, i) + (0,) * len(shape))
    st_out = lambda *shape: pl.BlockSpec((SB,) + shape, lambda i: (i,) + (0,) * len(shape))
    big_shape = jax.ShapeDtypeStruct((depth, ns, HEADS, HD, HD), F32)
    n_fixed = 16
    aliased = [] if big_states is None else list(big_states)
    return pl.pallas_call(
        _mix_sample_kernel,
        grid=(ns // SB,),
        in_specs=[
            pl.BlockSpec((SB, N_MIX), lambda i: (i, 0)),
            pl.BlockSpec((SB, LANES), lambda i: (i, 0)),
            pl.BlockSpec((1, HD), lambda i: (0, 0)),
            pl.BlockSpec((1, HD), lambda i: (0, 0)),
            full(GDN_CONV, 3 * BW), full(1, LANES), full(1, LANES), full(1, HD),
            full(SCONV_W, BW), full(HEADS, HD, HD), full(1, BW),
            st_in(HEADS, HD, HD), st_in(HEADS, HD, HD),
            st_in(GDN_CONV - 1, 3 * BW), st_in(SCONV_W - 1, BW), st_in(POOL_HIST, BW),
        ] + [pl.BlockSpec(memory_space=pl.ANY)] * len(aliased),
        out_specs=[
            pl.BlockSpec((N_BRANCH, SB, BW), lambda i: (0, i, 0)),
            st_in(HEADS, HD, HD), st_in(HEADS, HD, HD),
            st_out(GDN_CONV - 1, 3 * BW), st_out(SCONV_W - 1, BW), st_out(POOL_HIST, BW),
        ],
        out_shape=[
            jax.ShapeDtypeStruct((N_BRANCH, ns, BW), F32),
            big_shape, big_shape,
            jax.ShapeDtypeStruct((ns, GDN_CONV - 1, 3 * BW), F32),
            jax.ShapeDtypeStruct((ns, SCONV_W - 1, BW), F32),
            jax.ShapeDtypeStruct((ns, POOL_HIST, BW), F32),
        ],
        input_output_aliases={n_fixed + i: 1 + i for i in range(len(aliased))},
        scratch_shapes=[pltpu.VMEM((SB, 2 * BW), F32)],
        compiler_params=pltpu.CompilerParams(
            dimension_semantics=("arbitrary",), vmem_limit_bytes=VMEM_LIMIT),
        name="mix_sample",
    )(z, zab, cos_t, sin_t, gdn_conv_w, alog_v, dtb_v, gdn_norm_g, sconv_w, pool_w16, pool_scale,
      state_ret, state_gdn, state_gdn_conv, state_sconv, state_pool, *aliased)


N_CT = D_MODEL // WCOL
N_GATE_PHASES = N_CT * N_BRANCH
N_OUT_PHASES = N_GATE_PHASES + N_CT


def _out_kernel(x_ref, g_ref, ys_ref, wg_ref, wb_ref, wo_ref, xc_ref, o_ref, h_ref, acc_ref,
                merged_ref, *, row_splits):
    p = pl.program_id(1)
    n = p % N_BRANCH
    tm = x_ref.shape[0]
    rows = tm // row_splits

    @pl.when(p == 0)
    def _():
        h_ref[...] = (_rms_rows(x_ref[...]) * g_ref[...]).astype(BF16)
        acc_ref[...] = jnp.zeros_like(acc_ref)

    @pl.when(p < N_GATE_PHASES)
    def _():
        c = p // N_BRANCH
        for r in range(row_splits):
            rs = slice(r * rows, (r + 1) * rows)
            gate = _sigmoid(_dot(h_ref[rs, :], wg_ref[...]))
            contrib = gate * _dot(ys_ref[rs, :].astype(BF16), wb_ref[...])
            acc = jnp.where(n == 0, contrib, acc_ref[rs, :] + contrib)
            acc_ref[rs, :] = acc
            merged_ref[c, rs, :] = acc.astype(BF16)

    @pl.when(p >= N_GATE_PHASES)
    def _():
        acc = xc_ref[...]
        for c in range(N_CT):
            acc = acc + _dot(merged_ref[c], wo_ref[c * WCOL:(c + 1) * WCOL, :])
        o_ref[...] = acc


def _out_proj(x, norm_g, ys, w_g, w_b, w_o, layer, tm):
    m = x.shape[0]
    gate_phase = lambda p: jnp.minimum(p, N_GATE_PHASES - 1)
    branch = lambda p: gate_phase(p) % N_BRANCH
    col_tile = lambda p: gate_phase(p) // N_BRANCH
    out_tile = lambda p: jnp.maximum(p - N_GATE_PHASES, 0)
    return pl.pallas_call(
        functools.partial(_out_kernel, row_splits=2 if tm >= 256 else 1),
        grid=(m // tm, N_OUT_PHASES),
        in_specs=[
            pl.BlockSpec((tm, D_MODEL), lambda i, p: (i, 0)),
            pl.BlockSpec((None, 1, D_MODEL), lambda i, p: (layer, 0, 0)),
            pl.BlockSpec((None, tm, BW), lambda i, p: (branch(p), i, 0)),
            pl.BlockSpec((None, D_MODEL, WCOL),
                         lambda i, p: (layer, 0, branch(p) * N_CT + col_tile(p))),
            pl.BlockSpec((None, None, BW, WCOL), lambda i, p: (layer, branch(p), 0, col_tile(p))),
            pl.BlockSpec((None, D_MODEL, WCOL), lambda i, p: (layer, 0, out_tile(p))),
            pl.BlockSpec((tm, WCOL), lambda i, p: (i, out_tile(p))),
        ],
        out_specs=pl.BlockSpec((tm, WCOL), lambda i, p: (i, out_tile(p))),
        out_shape=jax.ShapeDtypeStruct((m, D_MODEL), F32),
        scratch_shapes=[pltpu.VMEM((tm, D_MODEL), BF16), pltpu.VMEM((tm, WCOL), F32),
                        pltpu.VMEM((N_CT, tm, WCOL), BF16)],
        compiler_params=pltpu.CompilerParams(
            dimension_semantics=("arbitrary", "arbitrary"), vmem_limit_bytes=VMEM_LIMIT),
        name="out_proj",
    )(x, norm_g, ys, w_g, w_b, w_o, x)


def _final_norm_kernel(x_ref, g_ref, o_ref):
    o_ref[...] = _rms_rows(x_ref[...]) * g_ref[...]


def _final_norm(x, g, tm):
    m = x.shape[0]
    return pl.pallas_call(
        _final_norm_kernel,
        grid=(m // tm,),
        in_specs=[pl.BlockSpec((tm, D_MODEL), lambda i: (i, 0)),
                  pl.BlockSpec((1, D_MODEL), lambda i: (0, 0))],
        out_specs=pl.BlockSpec((tm, D_MODEL), lambda i: (i, 0)),
        out_shape=jax.ShapeDtypeStruct((m, D_MODEL), F32),
        compiler_params=pltpu.CompilerParams(dimension_semantics=("arbitrary",)),
        name="final_norm",
    )(x, g)


def _cast_kernel(main_ref, o_ref):
    o_ref[...] = main_ref[...].astype(BF16)


def _cast_shift_kernel(main_ref, nxt_ref, o_ref, *, shift):
    main = pltpu.roll(main_ref[...], WCOL - shift, axis=1)
    nxt = pltpu.roll(nxt_ref[...], LANES - shift, axis=1)
    lane = lax.broadcasted_iota(jnp.int32, nxt.shape, 1)
    last = jnp.where(lane >= LANES - shift, nxt, main[:, WCOL - LANES:])
    o_ref[:, :WCOL - LANES] = main[:, :WCOL - LANES].astype(BF16)
    o_ref[:, WCOL - LANES:] = last.astype(BF16)


def _cast_columns(w, col0, ncols, tr=1024):
    depth, rows, _ = w.shape
    base_blk, shift = divmod(col0, WCOL)
    tiles = WCOL // LANES
    in_specs = [pl.BlockSpec((None, tr, WCOL), lambda l, r, j: (l, r, base_blk + j))]
    operands = [w]
    if shift:
        in_specs.append(pl.BlockSpec((None, tr, LANES),
                                     lambda l, r, j: (l, r, (base_blk + j + 1) * tiles)))
        operands.append(w)
        body = functools.partial(_cast_shift_kernel, shift=shift)
    else:
        body = _cast_kernel
    return pl.pallas_call(
        body,
        grid=(depth, rows // tr, ncols // WCOL),
        in_specs=in_specs,
        out_specs=pl.BlockSpec((None, tr, WCOL), lambda l, r, j: (l, r, j)),
        out_shape=jax.ShapeDtypeStruct((depth, rows, ncols), BF16),
        compiler_params=pltpu.CompilerParams(
            dimension_semantics=("arbitrary", "arbitrary", "arbitrary")),
        name="cast_columns",
    )(*operands)


def _rope_tables(pos):
    half = HD // 2
    inv = ROPE_BASE ** (-jnp.arange(half, dtype=F32) / half)
    ang = pos.astype(F32)[:, None] * inv[None, :]
    cos, sin = jnp.cos(ang), jnp.sin(ang)
    return jnp.concatenate([cos, cos], axis=1), jnp.concatenate([-sin, sin], axis=1)


def _prepare_weights(w_in, w_out):
    ab0 = 8 * BW
    g0 = N_MIX + 2 * HEADS
    w_za = _cast_columns(w_in, 0, ab0)
    w_zb = _cast_columns(w_in, ab0 + 2 * HEADS, N_MIX - ab0)
    w_ab = jnp.pad(w_in[:, :, ab0:ab0 + 2 * HEADS],
                   ((0, 0), (0, 0), (0, LANES - 2 * HEADS))).astype(BF16)
    w_g = _cast_columns(w_in, g0, N_GATE)
    return w_za, w_zb, w_ab, w_g, _cast_columns(w_out, 0, D_MODEL)


def kernel(x_prompt, x_sample, state_ret, state_gdn, state_gdn_conv, state_sconv, state_pool,
           norm_g, w_in, gdn_conv_w, gdn_a_log, gdn_dt_bias, gdn_norm_g, sconv_w, pool_w, pool_scale,
           w_branch, w_out, final_norm_g):
    bp, tp, _ = x_prompt.shape
    ns, ts, _ = x_sample.shape
    depth = w_in.shape[0]
    tt = min(256, tp)
    nt = tp // tt
    tm_in = min(1024, bp * tp)
    tm_out = min(512, bp * tp)

    w_za, w_zb, w_ab, w_g, w_o = _prepare_weights(w_in, w_out)
    w_b = w_branch.astype(BF16)
    pool_w16 = pool_w.astype(BF16)
    pad_h = lambda a: jnp.pad(a, ((0, 0), (0, LANES - HEADS)))[:, None, :]
    alog_v, dtb_v = pad_h(gdn_a_log), pad_h(gdn_dt_bias)
    norm_g3 = norm_g[:, None, :]
    gng3 = gdn_norm_g[:, None, :]
    pscale3 = pool_scale[:, None, :]

    cos_p, sin_p = _rope_tables(jnp.arange(tp, dtype=jnp.int32))
    cos_s, sin_s = _rope_tables(PAST_LEN + jnp.arange(ts, dtype=jnp.int32))

    hp = x_prompt.reshape(bp * tp, D_MODEL)
    hs = x_sample.reshape(ns * ts, D_MODEL)
    new_p = [[] for _ in range(5)]
    new_s = [[] for _ in range(3)]
    big_states = None
    for l in range(depth):
        z, zab = _in_proj(hp, norm_g3, w_za, w_zb, w_ab, l, tm=tm_in)
        ys, *st_p = _mix_prompt(z, zab, cos_p, sin_p, gdn_conv_w, alog_v, dtb_v, gng3, sconv_w,
                                pool_w16, pscale3, l, bp, nt, tt)
        hp = _out_proj(hp, norm_g3, ys, w_g, w_b, w_o, l, tm=tm_out)

        z, zab = _in_proj(hs, norm_g3, w_za, w_zb, w_ab, l, tm=ns)
        ys, s_ret, s_gdn, *st_s = _mix_sample(
            z, zab, cos_s, sin_s, gdn_conv_w, alog_v, dtb_v, gng3, sconv_w, pool_w16, pscale3,
            state_ret, state_gdn, state_gdn_conv, state_sconv, state_pool, l, big_states)
        big_states = (s_ret, s_gdn)
        hs = _out_proj(hs, norm_g3, ys, w_g, w_b, w_o, l, tm=ns)
        for i in range(5):
            new_p[i].append(st_p[i])
        for i in range(3):
            new_s[i].append(st_s[i])

    fg = final_norm_g[None, :]
    y_prompt = _final_norm(hp, fg, 512).reshape(bp, tp, D_MODEL)
    y_sample = _final_norm(hs, fg, ns).reshape(ns, ts, D_MODEL)
    outs_p = [jnp.stack(a, axis=0) for a in new_p]
    outs_s = [jnp.stack(a, axis=0) for a in new_s]
    return (y_prompt, y_sample, *outs_p, *big_states, *outs_s)
```
